```python
import jax
import jax.numpy as jnp
from jax import lax
import numpy as np


D_MODEL = 1024
BATCH = 4
SEQ = 4096
DEPTH = 2
DEC_BATCH = 128
DEC_SEQ = 8
PAST_LEN = 16384
PAGE_SIZE = 128

MLA_HEADS = 8
MLA_Q_LORA = 384
MLA_KV_LORA = 256
MLA_NOPE = 64
MLA_ROPE = 32
MLA_V = 64
ROPE_THETA = 10000.0
SB_HEADS = 8
SB_DIM = 64
EVEN_IN = MLA_Q_LORA + MLA_KV_LORA + MLA_ROPE + 3 * SB_HEADS * SB_DIM
EVEN_MIX = MLA_HEADS * MLA_V + SB_HEADS * SB_DIM
GLA_HEADS = 4
GLA_DK = D_MODEL // 2 // GLA_HEADS
GLA_DV = D_MODEL // GLA_HEADS
GLA_GATE_RANK = 16
GLA_TAU = 16.0
GLA_CHUNK = 32
ODD_IN = 2 * GLA_HEADS * GLA_DK + 2 * GLA_HEADS * GLA_DV + GLA_GATE_RANK
N_EXPERTS = 16
N_GROUPS = 4
TOP_K = 2
D_EXPERT = 512
Q_BLOCK = 128
N_EVEN = (DEPTH + 1) // 2
N_ODD = DEPTH // 2
DEEPNORM_ALPHA = (2.0 * DEPTH) ** 0.25
DEEPNORM_BETA = (8.0 * DEPTH) ** -0.25
NORM_EPS = 1e-6
LN_EPS = 1e-5

kernel_name = 'hybrid_mla_stickbreak_gla_moe_step'


def _rms_norm(x, g):
    xf = x.astype(jnp.float32)
    y = xf * lax.rsqrt(jnp.mean(xf * xf, axis=-1, keepdims=True) + NORM_EPS)
    return (y * g.astype(jnp.float32)).astype(x.dtype)


def _layer_norm(x, g, b):
    xf = x.astype(jnp.float32)
    mu = jnp.mean(xf, axis=-1, keepdims=True)
    var = jnp.mean(jnp.square(xf - mu), axis=-1, keepdims=True)
    y = (xf - mu) * lax.rsqrt(var + LN_EPS)
    return (y * g.astype(jnp.float32) + b.astype(jnp.float32)).astype(x.dtype)


def _rope(x, pos):
    half = MLA_ROPE // 2
    inv_freq = ROPE_THETA ** (-jnp.arange(half, dtype=jnp.float32) / half)
    ang = pos.astype(jnp.float32)[:, None] * inv_freq[None, :]
    shape = (1, pos.shape[0]) + (1,) * (x.ndim - 3) + (half,)
    cos = jnp.cos(ang).reshape(shape)
    sin = jnp.sin(ang).reshape(shape)
    xf = x.astype(jnp.float32)
    x1, x2 = xf[..., :half], xf[..., half:]
    return jnp.concatenate([x1 * cos - x2 * sin, x2 * cos + x1 * sin], axis=-1).astype(x.dtype)


def _gather_pages(pool, page_table):
    rows = pool[page_table]
    return rows.reshape((page_table.shape[0], page_table.shape[1] * pool.shape[1]) + pool.shape[2:])


def _sweep_query_blocks(fn, q_arrays, q_pos):
    t = q_pos.shape[0]
    blk = min(Q_BLOCK, t)
    n_blk = -(-t // blk)
    pad = n_blk * blk - t

    def to_blocks(a):
        a = jnp.pad(a, [(0, 0), (0, pad)] + [(0, 0)] * (a.ndim - 2))
        a = a.reshape((a.shape[0], n_blk, blk) + a.shape[2:])
        return jnp.swapaxes(a, 0, 1)

    qp = jnp.pad(q_pos, (0, pad), mode='edge').reshape(n_blk, blk)
    outs = lax.map(lambda args: fn(*args), tuple(to_blocks(a) for a in q_arrays) + (qp,))

    def from_blocks(o):
        o = jnp.swapaxes(o, 0, 1)
        return o.reshape((o.shape[0], n_blk * blk) + o.shape[3:])[:, :t]

    return [from_blocks(o) for o in outs]


def _mla_sb_block(q_lat, q_rope, sb_q, q_pos, ckv, krope, sb_k, sb_v, k_pos):
    s = jnp.einsum('bqhc,bkc->bhqk', q_lat, ckv) + jnp.einsum('bqhr,bkr->bhqk', q_rope, krope)
    s = s.astype(jnp.float32) * (MLA_NOPE + MLA_ROPE) ** -0.5
    causal = k_pos[None, :] <= q_pos[:, None]
    p = jax.nn.softmax(jnp.where(causal, s, -jnp.inf), axis=-1)
    o_lat = jnp.einsum('bhqk,bkc->bqhc', p.astype(ckv.dtype), ckv)
    z = jnp.einsum('bqhd,bkhd->bhqk', sb_q, sb_k).astype(jnp.float32) * SB_DIM ** -0.5
    strict = k_pos[None, :] < q_pos[:, None]
    log_stay = jnp.where(strict, jax.nn.log_sigmoid(-z), 0.0)
    log_after = lax.cumsum(log_stay, axis=3, reverse=True) - log_stay
    w = jnp.where(strict, jnp.exp(jax.nn.log_sigmoid(z) + log_after), 0.0)
    o_sb = jnp.einsum('bhqk,bkhd->bqhd', w.astype(sb_v.dtype), sb_v)
    return o_lat, o_sb


def _even_project(h, pos, w_in, q_norm_g, w_q_up, kv_norm_g, w_kv_up):
    b, t, _ = h.shape
    splits = np.cumsum([MLA_Q_LORA, MLA_KV_LORA, MLA_ROPE, SB_HEADS * SB_DIM, SB_HEADS * SB_DIM]).tolist()
    q_a, kv_a, k_r, sb_q, sb_k, sb_v = jnp.split(h @ w_in, splits, axis=-1)
    q = (_rms_norm(q_a, q_norm_g) @ w_q_up).reshape(b, t, MLA_HEADS, MLA_NOPE + MLA_ROPE)
    q_rope = _rope(q[..., MLA_NOPE:], pos)
    q_lat = jnp.einsum('bthn,chn->bthc', q[..., :MLA_NOPE], w_kv_up[..., :MLA_NOPE])
    ckv = _rms_norm(kv_a, kv_norm_g)
    krope = _rope(k_r, pos)
    sb_shape = (b, t, SB_HEADS, SB_DIM)
    return q_lat, q_rope, sb_q.reshape(sb_shape), ckv, krope, sb_k.reshape(sb_shape), sb_v.reshape(sb_shape)


def _even_mix(q_lat, q_rope, sb_q, q_pos, ckv, krope, sb_k, sb_v, k_pos, w_kv_up, w_out):
    def block_fn(ql, qr, sq, qp):
        return _mla_sb_block(ql, qr, sq, qp, ckv, krope, sb_k, sb_v, k_pos)

    o_lat, o_sb = _sweep_query_blocks(block_fn, (q_lat, q_rope, sb_q), q_pos)
    o_mla = jnp.einsum('bthc,chv->bthv', o_lat, w_kv_up[..., MLA_NOPE:])
    b, t = o_mla.shape[:2]
    o = jnp.concatenate([o_mla.reshape(b, t, MLA_HEADS * MLA_V), o_sb.reshape(b, t, SB_HEADS * SB_DIM)], axis=-1)
    return o @ w_out


def _gla_scan(q, k, v, log_a, s0):
    b, t, h, _ = q.shape
    dv = v.shape[-1]
    c = min(GLA_CHUNK, t)
    n = -(-t // c)
    pad = n * c - t

    def to_chunks(a):
        a = jnp.pad(a.astype(jnp.float32), ((0, 0), (0, pad), (0, 0), (0, 0)))
        return a.reshape(b, n, c, h, a.shape[-1]).transpose(1, 0, 3, 2, 4)

    incl = jnp.tril(jnp.ones((c, c), dtype=bool))

    def step(state, inp):
        qc, kc, vc, gc = inp
        g_cum = jnp.cumsum(gc, axis=2)
        g_last = g_cum[:, :, -1:, :]
        q_dec = qc * jnp.exp(g_cum)
        k_dec = kc * jnp.exp(-g_cum)
        att = jnp.where(incl, jnp.einsum('bhck,bhsk->bhcs', q_dec, k_dec), 0.0)
        o = jnp.einsum('bhck,bhkv->bhcv', q_dec, state) + jnp.einsum('bhcs,bhsv->bhcv', att, vc)
        k_to_end = kc * jnp.exp(g_last - g_cum)
        state = state * jnp.exp(g_last[:, :, 0, :])[..., None] + jnp.einsum('bhsk,bhsv->bhkv', k_to_end, vc)
        return state, o

    state, o = lax.scan(step, s0.astype(jnp.float32), tuple(to_chunks(a) for a in (q, k, v, log_a)))
    o = o.transpose(1, 0, 3, 2, 4).reshape(b, n * c, h, dv)[:, :t]
    return o.astype(v.dtype), state.astype(s0.dtype)


def _gla_mix(h, s0, w_in, w_gate_up, b_gate, norm_g, w_out):
    b, t, _ = h.shape
    hk = GLA_HEADS * GLA_DK
    hv = GLA_HEADS * GLA_DV
    q, k, v, r, g_lr = jnp.split(h @ w_in, [hk, 2 * hk, 2 * hk + hv, 2 * hk + 2 * hv], axis=-1)
    log_a = jax.nn.log_sigmoid((g_lr @ w_gate_up + b_gate).astype(jnp.float32)) / GLA_TAU
    qh = q.reshape(b, t, GLA_HEADS, GLA_DK) * GLA_DK ** -0.5
    o, s_new = _gla_scan(qh, k.reshape(b, t, GLA_HEADS, GLA_DK), v.reshape(b, t, GLA_HEADS, GLA_DV),
                         log_a.reshape(b, t, GLA_HEADS, GLA_DK), s0)
    o = _rms_norm(o, norm_g) * jax.nn.silu(r.reshape(b, t, GLA_HEADS, GLA_DV))
    return o.reshape(b, t, hv) @ w_out, s_new


def _moe(h, router_w, router_b, w_gate, w_up, w_down):
    b, t, d = h.shape
    x = h.reshape(b * t, d)
    n = x.shape[0]
    aff = jax.nn.sigmoid((x @ router_w).astype(jnp.float32))
    sel = (aff + router_b.astype(jnp.float32)).reshape(n, N_GROUPS, N_EXPERTS // N_GROUPS)
    group_score = jnp.sum(lax.top_k(sel, TOP_K)[0], axis=-1)
    in_group = jax.nn.one_hot(jnp.argmax(group_score, axis=-1), N_GROUPS, dtype=jnp.float32) > 0.5
    masked = jnp.where(in_group[:, :, None], sel, -jnp.inf).reshape(n, N_EXPERTS)
    _, top_idx = lax.top_k(masked, TOP_K)
    top_aff = jnp.take_along_axis(aff, top_idx, axis=-1)
    weights = top_aff / jnp.sum(top_aff, axis=-1, keepdims=True)
    gates = jnp.einsum('nk,nke->en', weights, jax.nn.one_hot(top_idx, N_EXPERTS, dtype=jnp.float32)).astype(x.dtype)

    def expert(acc, inp):
        wg, wu, wd, g = inp
        y = (jax.nn.silu(x @ wg) * (x @ wu)) @ wd
        return acc + g[:, None] * y, None

    out, _ = lax.scan(expert, jnp.zeros_like(x), (w_gate, w_up, w_down, gates))
    return out.reshape(b, t, d)


def _ada(c, w, b):
    mod = jax.nn.silu(c) @ w + b
    return [m[:, None, :] for m in jnp.split(mod, 6, axis=-1)]


def setup_inputs(seed: int = 0) -> dict:
    key = jax.random.key(seed)
    k = jax.random.split(key, 32)

    def nrm(i, shape, scale):
        return scale * jax.random.normal(k[i], shape, jnp.float32)

    def gain(i, shape):
        return 1.0 + nrm(i, shape, 0.02)

    n_pages = PAST_LEN // PAGE_SIZE
    n_used = DEC_BATCH * n_pages
    n_pool = n_used + (n_used + 3) // 4
    page_table = jax.random.permutation(k[10], n_pool)[:n_used].reshape(DEC_BATCH, n_pages).astype(jnp.int32)
    hv = GLA_HEADS * GLA_DV
    return {
        'x_prompt': nrm(0, (BATCH, SEQ, D_MODEL), 1.0),
        'x_sample': nrm(1, (DEC_BATCH, DEC_SEQ, D_MODEL), 1.0),
        'c_prompt': nrm(2, (BATCH, D_MODEL), 1.0),
        'c_sample': nrm(3, (DEC_BATCH, D_MODEL), 1.0),
        'cache_mla_ckv': nrm(4, (N_EVEN, n_pool, PAGE_SIZE, MLA_KV_LORA), 1.0),
        'cache_mla_krope': nrm(5, (N_EVEN, n_pool, PAGE_SIZE, MLA_ROPE), 1.0),
        'cache_sb_k': nrm(6, (N_EVEN, n_pool, PAGE_SIZE, SB_HEADS, SB_DIM), 1.0),
        'cache_sb_v': nrm(7, (N_EVEN, n_pool, PAGE_SIZE, SB_HEADS, SB_DIM), 1.0),
        'state_gla': nrm(8, (N_ODD, DEC_BATCH, GLA_HEADS, GLA_DK, GLA_DV), 0.5),
        'page_table': page_table,
        'w_in_even': nrm(11, (N_EVEN, D_MODEL, EVEN_IN), D_MODEL ** -0.5),
        'q_norm_g': gain(12, (N_EVEN, MLA_Q_LORA)),
        'w_q_up': nrm(13, (N_EVEN, MLA_Q_LORA, MLA_HEADS * (MLA_NOPE + MLA_ROPE)), MLA_Q_LORA ** -0.5),
        'kv_norm_g': gain(14, (N_EVEN, MLA_KV_LORA)),
        'w_kv_up': nrm(15, (N_EVEN, MLA_KV_LORA, MLA_HEADS, MLA_NOPE + MLA_V), MLA_KV_LORA ** -0.5),
        'w_out_even': nrm(16, (N_EVEN, EVEN_MIX, D_MODEL), DEEPNORM_BETA * EVEN_MIX ** -0.5),
        'w_in_odd': nrm(17, (N_ODD, D_MODEL, ODD_IN), D_MODEL ** -0.5),
        'w_gate_up': nrm(18, (N_ODD, GLA_GATE_RANK, GLA_HEADS * GLA_DK), GLA_GATE_RANK ** -0.5),
        'b_gate': nrm(19, (N_ODD, GLA_HEADS * GLA_DK), 0.1),
        'gla_norm_g': gain(20, (N_ODD, GLA_HEADS, GLA_DV)),
        'w_out_odd': nrm(21, (N_ODD, hv, D_MODEL), DEEPNORM_BETA * hv ** -0.5),
        'ada_w': nrm(22, (DEPTH, D_MODEL, 6 * D_MODEL), 0.1 * D_MODEL ** -0.5),
        'ada_b': nrm(23, (DEPTH, 6 * D_MODEL), 0.01),
        'ln_g': gain(24, (DEPTH, 2, D_MODEL)),
        'ln_b': nrm(25, (DEPTH, 2, D_MODEL), 0.01),
        'router_w': nrm(26, (D_MODEL, N_EXPERTS), D_MODEL ** -0.5),
        'router_b': nrm(27, (N_EXPERTS,), 0.01),
        'w_exp_gate': nrm(28, (DEPTH, N_EXPERTS, D_MODEL, D_EXPERT), D_MODEL ** -0.5),
        'w_exp_up': nrm(29, (DEPTH, N_EXPERTS, D_MODEL, D_EXPERT), D_MODEL ** -0.5),
        'w_exp_down': nrm(30, (DEPTH, N_EXPERTS, D_EXPERT, D_MODEL), DEEPNORM_BETA * D_EXPERT ** -0.5),
    }


def reference(x_prompt, x_sample, c_prompt, c_sample, cache_mla_ckv, cache_mla_krope, cache_sb_k, cache_sb_v,
              state_gla, page_table, w_in_even, q_norm_g, w_q_up, kv_norm_g, w_kv_up, w_out_even,
              w_in_odd, w_gate_up, b_gate, gla_norm_g, w_out_odd, ada_w, ada_b, ln_g, ln_b,
              router_w, router_b, w_exp_gate, w_exp_up, w_exp_down):
    past_len = page_table.shape[1] * cache_mla_ckv.shape[2]
    dec_seq = x_sample.shape[1]
    pos_p = jnp.arange(x_prompt.shape[1], dtype=jnp.int32)
    pos_s = past_len + jnp.arange(dec_seq, dtype=jnp.int32)
    k_pos_s = jnp.arange(past_len + dec_seq, dtype=jnp.int32)

    xp, xs = x_prompt, x_sample
    ckv_p_l, ckv_s_l, kr_p_l, kr_s_l = [], [], [], []
    sbk_p_l, sbk_s_l, sbv_p_l, sbv_s_l = [], [], [], []
    gla_p_l, gla_s_l = [], []
    for l in range(DEPTH):
        sh1p, sc1p, g1p, sh2p, sc2p, g2p = _ada(c_prompt, ada_w[l], ada_b[l])
        sh1s, sc1s, g1s, sh2s, sc2s, g2s = _ada(c_sample, ada_w[l], ada_b[l])
        hp = xp * (1.0 + sc1p) + sh1p
        hs = xs * (1.0 + sc1s) + sh1s
        if l % 2 == 0:
            e = l // 2
            proj_w = (w_in_even[e], q_norm_g[e], w_q_up[e], kv_norm_g[e], w_kv_up[e])
            qlp, qrp, sqp, ckv_p, kr_p, sk_p, sv_p = _even_project(hp, pos_p, *proj_w)
            qls, qrs, sqs, ckv_s, kr_s, sk_s, sv_s = _even_project(hs, pos_s, *proj_w)
            mix_p = _even_mix(qlp, qrp, sqp, pos_p, ckv_p, kr_p, sk_p, sv_p, pos_p, w_kv_up[e], w_out_even[e])

            def with_past(pool, new):
                return jnp.concatenate([_gather_pages(pool[e], page_table), new], axis=1)

            mix_s = _even_mix(qls, qrs, sqs, pos_s, with_past(cache_mla_ckv, ckv_s), with_past(cache_mla_krope, kr_s),
                              with_past(cache_sb_k, sk_s), with_past(cache_sb_v, sv_s), k_pos_s, w_kv_up[e], w_out_even[e])
            ckv_p_l.append(ckv_p); ckv_s_l.append(ckv_s)
            kr_p_l.append(kr_p); kr_s_l.append(kr_s)
            sbk_p_l.append(sk_p); sbk_s_l.append(sk_s)
            sbv_p_l.append(sv_p); sbv_s_l.append(sv_s)
        else:
            o = l // 2
            gla_w = (w_in_odd[o], w_gate_up[o], b_gate[o], gla_norm_g[o], w_out_odd[o])
            s0p = jnp.zeros((xp.shape[0], GLA_HEADS, GLA_DK, GLA_DV), state_gla.dtype)
            mix_p, sp = _gla_mix(hp, s0p, *gla_w)
            mix_s, ss = _gla_mix(hs, state_gla[o], *gla_w)
            gla_p_l.append(sp); gla_s_l.append(ss)
        xp = _layer_norm(DEEPNORM_ALPHA * xp + (1.0 + g1p) * mix_p, ln_g[l, 0], ln_b[l, 0])
        xs = _layer_norm(DEEPNORM_ALPHA * xs + (1.0 + g1s) * mix_s, ln_g[l, 0], ln_b[l, 0])
        moe_w = (router_w, router_b, w_exp_gate[l], w_exp_up[l], w_exp_down[l])
        ffn_p = _moe(xp * (1.0 + sc2p) + sh2p, *moe_w)
        ffn_s = _moe(xs * (1.0 + sc2s) + sh2s, *moe_w)
        xp = _layer_norm(DEEPNORM_ALPHA * xp + (1.0 + g2p) * ffn_p, ln_g[l, 1], ln_b[l, 1])
        xs = _layer_norm(DEEPNORM_ALPHA * xs + (1.0 + g2s) * ffn_s, ln_g[l, 1], ln_b[l, 1])

    return (xp, xs,
            jnp.stack(ckv_p_l), jnp.stack(ckv_s_l),
            jnp.stack(kr_p_l), jnp.stack(kr_s_l),
            jnp.stack(sbk_p_l), jnp.stack(sbk_s_l),
            jnp.stack(sbv_p_l), jnp.stack(sbv_s_l),
            jnp.stack(gla_p_l), jnp.stack(gla_s_l))
```

```python
import functools

import jax
import jax.numpy as jnp
from jax import lax
from jax.experimental import pallas as pl
from jax.experimental.pallas import tpu as pltpu

F32 = jnp.float32
BF16 = jnp.bfloat16

MLA_HEADS = 8
MLA_NOPE = 64
MLA_ROPE = 32
MLA_V = 64
ROPE_THETA = 10000.0
SB_HEADS = 8
SB_DIM = 64
GLA_HEADS = 4
GLA_TAU = 16.0
GLA_CHUNK = 32
N_EXPERTS = 16
N_GROUPS = 4
PER_GROUP = N_EXPERTS // N_GROUPS
NORM_EPS = 1e-6
LN_EPS = 1e-5

LANES = 128
ROPE_HALF = MLA_ROPE // 2
MLA_QPAD = 384
ROW_TILE = 256
MOE_ROW_TILE = 1024
ATT_TILE = 512
SB_TILE = 256
PAGES_PER_STEP = 16
VMEM_LIMIT = 56 * 1024 * 1024


def _params(*sem):
    return pltpu.CompilerParams(dimension_semantics=sem, vmem_limit_bytes=VMEM_LIMIT)


def _dot(a, b):
    return jnp.dot(a, b, preferred_element_type=F32)


def _dot_nt(a, b):
    return lax.dot_general(a, b, (((1,), (1,)), ((), ())), preferred_element_type=F32)


def _split_dot(x, m):
    hi = x.astype(BF16)
    lo = (x - hi.astype(F32)).astype(BF16)
    return _dot(hi, m) + _dot(lo, m)


def _sigmoid(x):
    return 1.0 / (1.0 + jnp.exp(-x))


def _silu(x):
    return x * _sigmoid(x)


def _rms(x, g):
    return x * lax.rsqrt(jnp.mean(x * x, axis=-1, keepdims=True) + NORM_EPS) * g


def _layer_norm(x, g, b):
    mu = jnp.mean(x, axis=-1, keepdims=True)
    xc = x - mu
    var = jnp.mean(xc * xc, axis=-1, keepdims=True)
    return xc * lax.rsqrt(var + LN_EPS) * g + b


def _ada_kernel(c_ref, w_ref, b_ref, o_ref):
    c = c_ref[...]
    o_ref[0] = _dot(_silu(c).astype(BF16), w_ref[0].astype(BF16)) + b_ref[0]


def _ada(c_all, ada_w, ada_b):
    depth, d, n6 = ada_w.shape
    rows = c_all.shape[0]
    tn = 1024
    return pl.pallas_call(
        _ada_kernel,
        grid=(depth, n6 // tn),
        in_specs=[
            pl.BlockSpec((rows, d), lambda l, n: (0, 0)),
            pl.BlockSpec((1, d, tn), lambda l, n: (l, 0, n)),
            pl.BlockSpec((1, 1, tn), lambda l, n: (l, 0, n)),
        ],
        out_specs=pl.BlockSpec((1, rows, tn), lambda l, n: (l, 0, n)),
        out_shape=jax.ShapeDtypeStruct((depth, rows, n6), F32),
        compiler_params=_params("arbitrary", "arbitrary"),
        name="ada",
    )(c_all, ada_w, ada_b.reshape(depth, 1, n6))


def _tile_rows(tile, n, seq_len, mod):
    return min(tile, seq_len if mod.ndim == 3 else n)


def _row_spec(r, c):
    return pl.BlockSpec((r, c), lambda i: (i, 0))


def _const_spec(shape):
    nd = len(shape)
    return pl.BlockSpec(shape, lambda i: (0,) * nd)


def _mod_spec(m, r, tiles_per_seq):
    if m.ndim == 3:
        return pl.BlockSpec((None, 1, m.shape[-1]), lambda i: (i // tiles_per_seq, 0, 0))
    return _row_spec(r, m.shape[-1])


def _tab_spec(t, r, tiles_per_seq, per_token):
    if per_token:
        return _row_spec(r, t.shape[-1])
    return pl.BlockSpec((r, t.shape[-1]), lambda i: (i % tiles_per_seq, 0))


def _even_proj_kernel(x_ref, sc_ref, sh_ref, cq_ref, sq_ref, ck_ref, skn_ref, skp_ref,
                      w1_ref, qg_ref, wq_ref, kvg_ref, wcat_ref,
                      qcat_ref, kcat_ref, ckv_ref, kr_ref, sbq_ref, sbk_ref, sbv_ref,
                      sbk16_ref, sbv16_ref, *, q_lora, kv_lora, sb_w):
    h = x_ref[...] * (1.0 + sc_ref[...]) + sh_ref[...]
    y = _dot(h.astype(BF16), w1_ref[...])
    o = 0
    q_a = y[:, o:o + q_lora]; o += q_lora
    kv_a = y[:, o:o + kv_lora]; o += kv_lora
    sbq = y[:, o:o + sb_w]; o += sb_w
    sbk = y[:, o:o + sb_w]; o += sb_w
    sbv = y[:, o:o + sb_w]; o += sb_w
    kr = y[:, o:o + LANES]

    q = _dot(_rms(q_a, qg_ref[...]).astype(BF16), wq_ref[...])
    n_nope = MLA_HEADS * MLA_NOPE
    x1 = q[:, n_nope:n_nope + LANES]
    x2 = q[:, n_nope + LANES:n_nope + 2 * LANES]
    cos, sin = cq_ref[...], sq_ref[...]
    qin = jnp.concatenate([q[:, :n_nope], x1 * cos - x2 * sin, x2 * cos + x1 * sin], axis=1)
    qcat_ref[...] = _dot(qin.astype(BF16), wcat_ref[...]).astype(BF16)

    ckv = _rms(kv_a, kvg_ref[...])
    ckv_ref[...] = ckv
    kro = (kr * ck_ref[...] + pltpu.roll(kr, LANES - ROPE_HALF, 1) * skn_ref[...]
           + pltpu.roll(kr, ROPE_HALF, 1) * skp_ref[...])
    kr_ref[...] = kro[:, :MLA_ROPE]
    kcat_ref[...] = jnp.concatenate([ckv, kro], axis=1).astype(BF16)

    sbq_ref[...] = sbq.astype(BF16)
    sbk_ref[...] = sbk
    sbv_ref[...] = sbv
    sbk16_ref[...] = sbk.astype(BF16)
    sbv16_ref[...] = sbv.astype(BF16)


def _even_proj(x, sc, sh, tabs, per_token, seq_len, w1, qg, wq, kvg, wcat):
    n, d = x.shape
    r = _tile_rows(ROW_TILE, n, seq_len, sc)
    tps = max(seq_len // r, 1)
    q_lora, kv_lora = qg.shape[-1], kvg.shape[-1]
    sb_w = SB_HEADS * SB_DIM
    qw = MLA_HEADS * MLA_QPAD
    kern = functools.partial(_even_proj_kernel, q_lora=q_lora, kv_lora=kv_lora, sb_w=sb_w)
    out_shapes = [
        jax.ShapeDtypeStruct((n, qw), BF16),
        jax.ShapeDtypeStruct((n, kv_lora + LANES), BF16),
        jax.ShapeDtypeStruct((n, kv_lora), F32),
        jax.ShapeDtypeStruct((n, MLA_ROPE), F32),
        jax.ShapeDtypeStruct((n, sb_w), BF16),
        jax.ShapeDtypeStruct((n, sb_w), F32),
        jax.ShapeDtypeStruct((n, sb_w), F32),
        jax.ShapeDtypeStruct((n, sb_w), BF16),
        jax.ShapeDtypeStruct((n, sb_w), BF16),
    ]
    return pl.pallas_call(
        kern,
        grid=(n // r,),
        in_specs=[_row_spec(r, d), _mod_spec(sc, r, tps), _mod_spec(sh, r, tps)]
        + [_tab_spec(t, r, tps, per_token) for t in tabs]
        + [_const_spec(w1.shape), _const_spec(qg.shape), _const_spec(wq.shape),
           _const_spec(kvg.shape), _const_spec(wcat.shape)],
        out_specs=[_row_spec(r, s.shape[1]) for s in out_shapes],
        out_shape=out_shapes,
        compiler_params=_params("arbitrary"),
        name="even_proj",
    )(x, sc, sh, *tabs, w1, qg, wq, kvg, wcat)


def _mla_prompt_kernel(q_ref, k_ref, o_ref, m_sc, l_sc, acc_sc, *, tile, kv_lora, scale):
    qi = pl.program_id(2)
    q = q_ref[...]
    m_sc[...] = jnp.full(m_sc.shape, -jnp.inf, F32)
    l_sc[...] = jnp.zeros(l_sc.shape, F32)
    acc_sc[...] = jnp.zeros(acc_sc.shape, F32)

    def block(kb, masked):
        k = k_ref[pl.ds(pl.multiple_of(kb * tile, tile), tile), :]
        s = _dot_nt(q, k) * scale
        if masked:
            row = lax.broadcasted_iota(jnp.int32, s.shape, 0)
            col = lax.broadcasted_iota(jnp.int32, s.shape, 1)
            s = jnp.where(col <= row, s, -jnp.inf)
        m_prev = m_sc[...]
        m_new = jnp.maximum(m_prev, jnp.max(s, axis=1, keepdims=True))
        alpha = jnp.exp(m_prev - m_new)
        p = jnp.exp(s - m_new)
        l_sc[...] = alpha * l_sc[...] + jnp.sum(p, axis=1, keepdims=True)
        acc_sc[...] = alpha * acc_sc[...] + _dot(p.astype(BF16), k[:, :kv_lora])
        m_sc[...] = m_new

    def body(kb, carry):
        block(kb, False)
        return carry

    lax.fori_loop(0, qi, body, 0)
    block(qi, True)
    o_ref[...] = (acc_sc[...] / l_sc[...]).astype(BF16)


def _mla_prompt(qcat, kcat, batch, seq):
    n = qcat.shape[0]
    kv_lora = kcat.shape[1] - LANES
    tile = min(ATT_TILE, seq)
    nq = seq // tile
    scale = (MLA_NOPE + MLA_ROPE) ** -0.5
    kern = functools.partial(_mla_prompt_kernel, tile=tile, kv_lora=kv_lora, scale=scale)
    return pl.pallas_call(
        kern,
        grid=(batch, MLA_HEADS, nq),
        in_specs=[
            pl.BlockSpec((tile, MLA_QPAD), lambda b, h, i: (b * nq + i, h)),
            pl.BlockSpec((seq, kcat.shape[1]), lambda b, h, i: (b, 0)),
        ],
        out_specs=pl.BlockSpec((tile, kv_lora), lambda b, h, i: (b * nq + i, h)),
        out_shape=jax.ShapeDtypeStruct((n, MLA_HEADS * kv_lora), BF16),
        scratch_shapes=[pltpu.VMEM((tile, 1), F32), pltpu.VMEM((tile, 1), F32),
                        pltpu.VMEM((tile, kv_lora), F32)],
        compiler_params=_params("arbitrary", "arbitrary", "arbitrary"),
        name="mla_prompt",
    )(qcat, kcat)


def _suffix_matrix(n):
    row = lax.broadcasted_iota(jnp.int32, (n, n), 0)
    col = lax.broadcasted_iota(jnp.int32, (n, n), 1)
    return jnp.where(row > col, 1.0, 0.0).astype(BF16)


def _sb_block(z, valid, carry, suffix):
    t = jnp.log1p(jnp.exp(-jnp.abs(z)))
    log_beta = jnp.minimum(z, 0.0) - t
    log_stay = log_beta - z
    if valid is not None:
        log_stay = jnp.where(valid, log_stay, 0.0)
    log_after = carry + _split_dot(log_stay, suffix)
    w = jnp.exp(log_beta + log_after)
    if valid is not None:
        w = jnp.where(valid, w, 0.0)
    return w, carry + jnp.sum(log_stay, axis=1, keepdims=True)


def _sb_prompt_kernel(q_ref, k_ref, v_ref, o_ref, *, tile, scale):
    qi = pl.program_id(2)
    suffix = _suffix_matrix(tile)
    row = lax.broadcasted_iota(jnp.int32, (tile, tile), 0)
    col = lax.broadcasted_iota(jnp.int32, (tile, tile), 1)
    strict = col < row
    heads = LANES // SB_DIM
    qs = [q_ref[:, hh * SB_DIM:(hh + 1) * SB_DIM] for hh in range(heads)]

    def block(kb, state, valid):
        start = pl.multiple_of(kb * tile, tile)
        k2 = k_ref[pl.ds(start, tile), :]
        v2 = v_ref[pl.ds(start, tile), :]
        new = []
        for hh in range(heads):
            carry, acc = state[hh]
            sl = slice(hh * SB_DIM, (hh + 1) * SB_DIM)
            w, carry = _sb_block(_dot_nt(qs[hh], k2[:, sl]) * scale, valid, carry, suffix)
            new.append((carry, acc + _dot(w.astype(BF16), v2[:, sl])))
        return tuple(new)

    state = tuple((jnp.zeros((tile, 1), F32), jnp.zeros((tile, SB_DIM), F32)) for _ in range(heads))
    state = block(qi, state, strict)
    state = lax.fori_loop(0, qi, lambda i, st: block(qi - 1 - i, st, None), state)
    o_ref[...] = jnp.concatenate([acc for _, acc in state], axis=1).astype(BF16)


def _sb_prompt(sbq, sbk, sbv, batch, seq):
    n, w = sbq.shape
    tile = min(SB_TILE, seq)
    nq = seq // tile
    kern = functools.partial(_sb_prompt_kernel, tile=tile, scale=SB_DIM ** -0.5)
    return pl.pallas_call(
        kern,
        grid=(batch, w // LANES, nq),
        in_specs=[
            pl.BlockSpec((tile, LANES), lambda b, h, i: (b * nq + i, h)),
            pl.BlockSpec((seq, LANES), lambda b, h, i: (b, h)),
            pl.BlockSpec((seq, LANES), lambda b, h, i: (b, h)),
        ],
        out_specs=pl.BlockSpec((tile, LANES), lambda b, h, i: (b * nq + i, h)),
        out_shape=jax.ShapeDtypeStruct((n, w), BF16),
        compiler_params=_params("arbitrary", "arbitrary", "arbitrary"),
        name="sb_prompt",
    )(sbq, sbk, sbv)


def _pad_rows(x, rows):
    return jnp.concatenate([x, jnp.zeros((rows - x.shape[0], x.shape[1]), x.dtype)], axis=0)


def _mla_sample_kernel(pt_ref, q_ref, kn_ref, *refs, n_pages, page, kv_lora, scale, dec):
    del pt_ref
    ckv_refs = refs[:n_pages]
    krt_refs = refs[n_pages:2 * n_pages]
    o_ref = refs[2 * n_pages]
    q_sc, kv_sc, krt_sc, m_sc, l_sc, acc_sc = refs[2 * n_pages + 1:]
    g = pl.program_id(1)
    rows = MLA_HEADS * dec

    def update(s, values):
        m_prev = m_sc[...]
        m_new = jnp.maximum(m_prev, jnp.max(s, axis=1, keepdims=True))
        alpha = jnp.exp(m_prev - m_new)
        p = jnp.exp(s - m_new)
        l_sc[...] = alpha * l_sc[...] + jnp.sum(p, axis=1, keepdims=True)
        acc_sc[...] = alpha * acc_sc[...] + _dot(p.astype(BF16), values)
        m_sc[...] = m_new

    @pl.when(g == 0)
    def _():
        qf = q_ref[0].astype(F32)
        q = jnp.concatenate([qf[:, h * MLA_QPAD:(h + 1) * MLA_QPAD] for h in range(MLA_HEADS)], axis=0)
        q = q.astype(BF16)
        q_sc[...] = q
        m_sc[...] = jnp.full(m_sc.shape, -jnp.inf, F32)
        l_sc[...] = jnp.zeros(l_sc.shape, F32)
        acc_sc[...] = jnp.zeros(acc_sc.shape, F32)
        kn = _pad_rows(kn_ref[0].astype(F32), page).astype(BF16)
        s = _dot_nt(q, kn) * scale
        t = lax.broadcasted_iota(jnp.int32, (rows, page), 0) % dec
        j = lax.broadcasted_iota(jnp.int32, (rows, page), 1)
        update(jnp.where(j <= t, s, -jnp.inf), kn[:, :kv_lora])

    for p in range(n_pages):
        kv_sc[p * page:(p + 1) * page, :] = ckv_refs[p][...].astype(BF16)
        krt_sc[:, p * page:(p + 1) * page] = krt_refs[p][...].astype(BF16)
    q = q_sc[...]
    kv = kv_sc[...]
    s = _dot_nt(q[:, :kv_lora], kv) + _dot(q[:, kv_lora:kv_lora + MLA_ROPE], krt_sc[...])
    update(s * scale, kv)

    @pl.when(g == pl.num_programs(1) - 1)
    def _():
        o = acc_sc[...] / l_sc[...]
        o_ref[0] = jnp.concatenate([o[h * dec:(h + 1) * dec] for h in range(MLA_HEADS)], axis=1).astype(BF16)


def _page_specs(layer, n_pages, pages_total, block):
    specs = []
    for p in range(n_pages):
        def index(b, g, pt, p=p):
            return (layer, pt[b * pages_total + pages_total - 1 - (g * n_pages + p)], 0, 0)
        specs.append(pl.BlockSpec((None, None) + block, index))
    return specs


def _mla_sample(qcat, kcat_new, layer, pool_ckv, pool_krt, page_table, dec):
    n, qw = qcat.shape
    dec_batch, pages_total = page_table.shape
    page, kv_lora = pool_ckv.shape[2:]
    n_pages = min(PAGES_PER_STEP, pages_total)
    rows = MLA_HEADS * dec
    kern = functools.partial(_mla_sample_kernel, n_pages=n_pages, page=page, kv_lora=kv_lora,
                             scale=(MLA_NOPE + MLA_ROPE) ** -0.5, dec=dec)
    grid_spec = pltpu.PrefetchScalarGridSpec(
        num_scalar_prefetch=1,
        grid=(dec_batch, pages_total // n_pages),
        in_specs=[
            pl.BlockSpec((1, dec, qw), lambda b, g, pt: (b, 0, 0)),
            pl.BlockSpec((1, dec, kcat_new.shape[1]), lambda b, g, pt: (b, 0, 0)),
        ] + _page_specs(layer, n_pages, pages_total, (page, kv_lora))
        + _page_specs(layer, n_pages, pages_total, (MLA_ROPE, page)),
        out_specs=pl.BlockSpec((1, dec, MLA_HEADS * kv_lora), lambda b, g, pt: (b, 0, 0)),
        scratch_shapes=[pltpu.VMEM((rows, MLA_QPAD), BF16),
                        pltpu.VMEM((n_pages * page, kv_lora), BF16),
                        pltpu.VMEM((MLA_ROPE, n_pages * page), BF16),
                        pltpu.VMEM((rows, 1), F32), pltpu.VMEM((rows, 1), F32),
                        pltpu.VMEM((rows, kv_lora), F32)],
    )
    out = pl.pallas_call(
        kern,
        grid_spec=grid_spec,
        out_shape=jax.ShapeDtypeStruct((dec_batch, dec, MLA_HEADS * kv_lora), BF16),
        compiler_params=_params("arbitrary", "arbitrary"),
        name="mla_sample",
    )(page_table.reshape(-1), qcat.reshape(dec_batch, dec, qw),
      kcat_new.reshape(dec_batch, dec, -1), *([pool_ckv] * n_pages), *([pool_krt] * n_pages))
    return out.reshape(n, MLA_HEADS * kv_lora)


def _sb_sample_kernel(pt_ref, q_ref, kn_ref, vn_ref, *refs, n_pages, page, scale, dec):
    del pt_ref
    k_refs = refs[:n_pages]
    v_refs = refs[n_pages:2 * n_pages]
    o_ref = refs[2 * n_pages]
    q_sc, k_sc, v_sc, carry_sc, acc_sc = refs[2 * n_pages + 1:]
    g = pl.program_id(1)
    rows = SB_HEADS * dec
    width = SB_HEADS * SB_DIM
    suffix = _suffix_matrix(page)

    def head_mask(shape):
        r = lax.broadcasted_iota(jnp.int32, shape, 0)
        c = lax.broadcasted_iota(jnp.int32, shape, 1)
        return (r // dec) == (c // SB_DIM)

    def weights(z, valid, n):
        zr = jnp.concatenate([z[:, c * page:(c + 1) * page] for c in range(n)], axis=0)
        t = jnp.log1p(jnp.exp(-jnp.abs(zr)))
        log_beta = jnp.minimum(zr, 0.0) - t
        log_stay = log_beta - zr
        if valid is not None:
            log_stay = jnp.where(valid, log_stay, 0.0)
        within = _split_dot(log_stay, suffix)
        total = jnp.sum(log_stay, axis=1, keepdims=True)
        carry = carry_sc[...]
        carries = []
        for c in range(n):
            carries.append(carry)
            carry = carry + total[c * rows:(c + 1) * rows]
        carry_sc[...] = carry
        w = jnp.exp(log_beta + within + jnp.concatenate(carries, axis=0))
        if valid is not None:
            w = jnp.where(valid, w, 0.0)
        return jnp.concatenate([w[c * rows:(c + 1) * rows] for c in range(n)], axis=1).astype(BF16)

    @pl.when(g == 0)
    def _():
        qf = q_ref[0].astype(F32)
        q = jnp.concatenate([qf] * SB_HEADS, axis=0)
        q = jnp.where(head_mask((rows, width)), q, 0.0).astype(BF16)
        q_sc[...] = q
        carry_sc[...] = jnp.zeros(carry_sc.shape, F32)
        kn = _pad_rows(kn_ref[0].astype(F32), page).astype(BF16)
        vn = _pad_rows(vn_ref[0].astype(F32), page).astype(BF16)
        t = lax.broadcasted_iota(jnp.int32, (rows, page), 0) % dec
        j = lax.broadcasted_iota(jnp.int32, (rows, page), 1)
        w = weights(_dot_nt(q, kn) * scale, j < t, 1)
        acc_sc[...] = _dot(w, vn)

    for p in range(n_pages):
        k_sc[:, p * page:(p + 1) * page] = k_refs[p][...].astype(BF16)
        v_sc[:, p * page:(p + 1) * page] = v_refs[p][...].astype(BF16)
    w = weights(_dot(q_sc[...], k_sc[...]) * scale, None, n_pages)
    acc_sc[...] += _dot_nt(w, v_sc[...])

    @pl.when(g == pl.num_programs(1) - 1)
    def _():
        acc = jnp.where(head_mask((rows, width)), acc_sc[...], 0.0)
        o = acc[0:dec]
        for h in range(1, SB_HEADS):
            o = o + acc[h * dec:(h + 1) * dec]
        o_ref[0] = o.astype(BF16)


def _sb_sample(sbq, sbk_new, sbv_new, layer, pool_kt, pool_vt, page_table, dec):
    n, width = sbq.shape
    dec_batch, pages_total = page_table.shape
    page = pool_kt.shape[3]
    n_pages = min(PAGES_PER_STEP, pages_total)
    rows = SB_HEADS * dec
    kern = functools.partial(_sb_sample_kernel, n_pages=n_pages, page=page, scale=SB_DIM ** -0.5, dec=dec)
    new_spec = pl.BlockSpec((1, dec, width), lambda b, g, pt: (b, 0, 0))
    grid_spec = pltpu.PrefetchScalarGridSpec(
        num_scalar_prefetch=1,
        grid=(dec_batch, pages_total // n_pages),
        in_specs=[new_spec, new_spec, new_spec]
        + _page_specs(layer, n_pages, pages_total, (width, page))
        + _page_specs(layer, n_pages, pages_total, (width, page)),
        out_specs=new_spec,
        scratch_shapes=[pltpu.VMEM((rows, width), BF16),
                        pltpu.VMEM((width, n_pages * page), BF16),
                        pltpu.VMEM((width, n_pages * page), BF16),
                        pltpu.VMEM((rows, 1), F32),
                        pltpu.VMEM((rows, width), F32)],
    )
    out = pl.pallas_call(
        kern,
        grid_spec=grid_spec,
        out_shape=jax.ShapeDtypeStruct((dec_batch, dec, width), BF16),
        compiler_params=_params("arbitrary", "arbitrary"),
        name="sb_sample",
    )(page_table.reshape(-1), sbq.reshape(dec_batch, dec, width), sbk_new.reshape(dec_batch, dec, width),
      sbv_new.reshape(dec_batch, dec, width), *([pool_kt] * n_pages), *([pool_vt] * n_pages))
    return out.reshape(n, width)


def _router_gates(h2, rw_ref, rb_ref):
    logits = lax.dot_general(rw_ref[...], h2, (((1,), (1,)), ((), ())),
                             precision=lax.Precision.HIGHEST, preferred_element_type=F32)
    aff = _sigmoid(logits)
    sel = aff + rb_ref[...]
    per = PER_GROUP
    s = [sel[j * N_GROUPS:(j + 1) * N_GROUPS] for j in range(per)]
    a = [aff[j * N_GROUPS:(j + 1) * N_GROUPS] for j in range(per)]
    score = None
    for i in range(per):
        for j in range(i + 1, per):
            pair = s[i] + s[j]
            score = pair if score is None else jnp.maximum(score, pair)
    best = jnp.max(score, axis=0, keepdims=True)
    gidx = lax.broadcasted_iota(jnp.int32, score.shape, 0)
    first = jnp.min(jnp.where(score == best, gidx, N_GROUPS), axis=0, keepdims=True)
    in_group = gidx == first
    picked = []
    for j in range(per):
        rank = jnp.zeros(score.shape, F32)
        for k in range(per):
            if k == j:
                continue
            ahead = (s[k] >= s[j]) if k < j else (s[k] > s[j])
            rank = rank + jnp.where(ahead, 1.0, 0.0)
        picked.append(jnp.where(in_group, jnp.where(rank < 2.0, a[j], 0.0), 0.0))
    total = None
    for j in range(per):
        part = jnp.sum(picked[j], axis=0, keepdims=True)
        total = part if total is None else total + part
    gates_t = jnp.concatenate([p / total for p in picked], axis=0)
    gates_t = _pad_rows(gates_t, LANES)
    return gates_t.T


def _mix_tail(x, mix, g1, sc2, sh2, lng_ref, lnb_ref, rw_ref, rb_ref, alpha, x1_ref, h2_ref, gates_ref):
    x1 = _layer_norm(alpha * x + (1.0 + g1) * mix, lng_ref[...], lnb_ref[...])
    h2 = x1 * (1.0 + sc2) + sh2
    x1_ref[...] = x1
    h2_ref[...] = h2.astype(BF16)
    gates_ref[...] = _router_gates(h2, rw_ref, rb_ref)


def _even_post_kernel(olat_ref, osb_ref, x_ref, g1_ref, sc2_ref, sh2_ref, bdv_ref, wo_ref,
                      lng_ref, lnb_ref, rw_ref, rb_ref, x1_ref, h2_ref, gates_ref, *, alpha):
    o_mla = _dot(olat_ref[...], bdv_ref[...]).astype(BF16)
    nv = o_mla.shape[1]
    mix = _dot(o_mla, wo_ref[:nv, :]) + _dot(osb_ref[...], wo_ref[nv:, :])
    _mix_tail(x_ref[...], mix, g1_ref[...], sc2_ref[...], sh2_ref[...], lng_ref, lnb_ref,
              rw_ref, rb_ref, alpha, x1_ref, h2_ref, gates_ref)


def _odd_post_kernel(o_ref, r_ref, x_ref, g1_ref, sc2_ref, sh2_ref, ng_ref, wo_ref,
                     lng_ref, lnb_ref, rw_ref, rb_ref, x1_ref, h2_ref, gates_ref, *, alpha, dv):
    o = o_ref[...]
    parts = []
    for h in range(GLA_HEADS):
        sl = slice(h * dv, (h + 1) * dv)
        parts.append(_rms(o[:, sl], ng_ref[:, sl]))
    gated = jnp.concatenate(parts, axis=1) * _silu(r_ref[...])
    mix = _dot(gated.astype(BF16), wo_ref[...])
    _mix_tail(x_ref[...], mix, g1_ref[...], sc2_ref[...], sh2_ref[...], lng_ref, lnb_ref,
              rw_ref, rb_ref, alpha, x1_ref, h2_ref, gates_ref)


def _post_call(kern, name, acts, x, mods, consts, seq_len):
    n, d = x.shape
    r = _tile_rows(ROW_TILE, n, seq_len, mods[0])
    tps = max(seq_len // r, 1)
    out_shapes = [jax.ShapeDtypeStruct((n, d), F32), jax.ShapeDtypeStruct((n, d), BF16),
                  jax.ShapeDtypeStruct((n, LANES), F32)]
    return pl.pallas_call(
        kern,
        grid=(n // r,),
        in_specs=[_row_spec(r, a.shape[1]) for a in acts] + [_row_spec(r, d)]
        + [_mod_spec(m, r, tps) for m in mods] + [_const_spec(c.shape) for c in consts],
        out_specs=[_row_spec(r, s.shape[1]) for s in out_shapes],
        out_shape=out_shapes,
        compiler_params=_params("arbitrary"),
        name=name,
    )(*acts, x, *mods, *consts)


def _moe_kernel(h_ref, gates_ref, x1_ref, g2_ref, wg_ref, wu_ref, wd_ref, lng_ref, lnb_ref,
                o_ref, acc_sc, *, alpha):
    e = pl.program_id(1)

    @pl.when(e == 0)
    def _():
        acc_sc[...] = jnp.zeros(acc_sc.shape, F32)

    h = h_ref[...]
    a = _silu(_dot(h, wg_ref[0])) * _dot(h, wu_ref[0])
    y = _dot(a.astype(BF16), wd_ref[0])
    col = (e % PER_GROUP) * N_GROUPS + e // PER_GROUP
    gates = gates_ref[...]
    lane = lax.broadcasted_iota(jnp.int32, gates.shape, 1)
    gate = jnp.sum(jnp.where(lane == col, gates, 0.0), axis=1, keepdims=True)
    acc_sc[...] += gate * y

    @pl.when(e == pl.num_programs(1) - 1)
    def _():
        o_ref[...] = _layer_norm(alpha * x1_ref[...] + (1.0 + g2_ref[...]) * acc_sc[...],
                                 lng_ref[...], lnb_ref[...])


def _moe(h2, gates, x1, g2, wg, wu, wd, lng, lnb, seq_len, alpha):
    n, d = x1.shape
    r = _tile_rows(MOE_ROW_TILE, n, seq_len, g2)
    tps = max(seq_len // r, 1)
    n_exp, _, de = wg.shape
    if g2.ndim == 3:
        g2_spec = pl.BlockSpec((None, 1, d), lambda i, e: (i // tps, 0, 0))
    else:
        g2_spec = pl.BlockSpec((r, d), lambda i, e: (i, 0))
    return pl.pallas_call(
        functools.partial(_moe_kernel, alpha=alpha),
        grid=(n // r, n_exp),
        in_specs=[
            pl.BlockSpec((r, d), lambda i, e: (i, 0)),
            pl.BlockSpec((r, LANES), lambda i, e: (i, 0)),
            pl.BlockSpec((r, d), lambda i, e: (i, 0)),
            g2_spec,
            pl.BlockSpec((1, d, de), lambda i, e: (e, 0, 0)),
            pl.BlockSpec((1, d, de), lambda i, e: (e, 0, 0)),
            pl.BlockSpec((1, de, d), lambda i, e: (e, 0, 0)),
            pl.BlockSpec((1, d), lambda i, e: (0, 0)),
            pl.BlockSpec((1, d), lambda i, e: (0, 0)),
        ],
        out_specs=pl.BlockSpec((r, d), lambda i, e: (i, 0)),
        out_shape=jax.ShapeDtypeStruct((n, d), F32),
        scratch_shapes=[pltpu.VMEM((r, d), F32)],
        compiler_params=_params("arbitrary", "arbitrary"),
        name="moe",
    )(h2, gates, x1, g2, wg, wu, wd, lng, lnb)


def _odd_proj_kernel(x_ref, sc_ref, sh_ref, w_ref, wkt_ref, wgu_ref, wgut_ref, bg_ref, bgt_ref,
                     qd_ref, kdt_ref, ket_ref, dec_ref, v_ref, r_ref, *, hk, hv, chunk, qscale):
    h = (x_ref[...] * (1.0 + sc_ref[...]) + sh_ref[...]).astype(BF16)
    rows = h.shape[0]
    y = _dot(h, w_ref[...])
    q = y[:, :hk] * qscale
    v = y[:, hk:hk + hv]
    r = y[:, hk + hv:hk + 2 * hv]
    glr = y[:, hk + 2 * hv:hk + 2 * hv + LANES].astype(BF16)
    kt = _dot_nt(wkt_ref[...], h)

    def log_sigmoid(z):
        return jnp.minimum(z, 0.0) - jnp.log1p(jnp.exp(-jnp.abs(z)))

    log_a = log_sigmoid(_dot(glr, wgu_ref[...]) + bg_ref[...]) * (1.0 / GLA_TAU)
    log_at = log_sigmoid(_dot_nt(wgut_ref[...], glr) + bgt_ref[...]) * (1.0 / GLA_TAU)

    i = lax.broadcasted_iota(jnp.int32, (rows, rows), 0)
    j = lax.broadcasted_iota(jnp.int32, (rows, rows), 1)
    same = (i // chunk) == (j // chunk)
    lower = jnp.where(same & (j <= i), 1.0, 0.0).astype(BF16)
    upper = jnp.where(same & (i <= j), 1.0, 0.0).astype(BF16)
    block = jnp.where(same, 1.0, 0.0).astype(BF16)

    hi = log_a.astype(BF16)
    lo = (log_a - hi.astype(F32)).astype(BF16)
    g_cum = _dot(lower, hi) + _dot(lower, lo)
    g_cum_t = _split_dot(log_at, upper)
    g_tot_t = _split_dot(log_at, block)

    qd_ref[...] = (q * jnp.exp(g_cum)).astype(BF16)
    kdt_ref[...] = (kt * jnp.exp(-g_cum_t)).astype(BF16)
    ket_ref[...] = (kt * jnp.exp(g_tot_t - g_cum_t)).astype(BF16)
    dec_ref[...] = jnp.exp(g_tot_t)
    v_ref[...] = v.astype(BF16)
    r_ref[...] = r


def _odd_proj(x, sc, sh, seq_len, w, wkt, wgu, wgut, bg, bgt, chunk):
    n, d = x.shape
    r = _tile_rows(ROW_TILE, n, seq_len, sc)
    tps = max(seq_len // r, 1)
    hk = wkt.shape[0]
    hv = (w.shape[1] - hk - LANES) // 2
    kern = functools.partial(_odd_proj_kernel, hk=hk, hv=hv, chunk=chunk, qscale=(hk // GLA_HEADS) ** -0.5)
    col_spec = pl.BlockSpec((hk, r), lambda i: (0, i))
    out_shapes = [
        jax.ShapeDtypeStruct((n, hk), BF16),
        jax.ShapeDtypeStruct((hk, n), BF16),
        jax.ShapeDtypeStruct((hk, n), BF16),
        jax.ShapeDtypeStruct((hk, n), F32),
        jax.ShapeDtypeStruct((n, hv), BF16),
        jax.ShapeDtypeStruct((n, hv), F32),
    ]
    return pl.pallas_call(
        kern,
        grid=(n // r,),
        in_specs=[_row_spec(r, d), _mod_spec(sc, r, tps), _mod_spec(sh, r, tps)]
        + [_const_spec(c.shape) for c in (w, wkt, wgu, wgut, bg, bgt)],
        out_specs=[_row_spec(r, hk), col_spec, col_spec, col_spec, _row_spec(r, hv), _row_spec(r, hv)],
        out_shape=out_shapes,
        compiler_params=_params("arbitrary"),
        name="odd_proj",
    )(x, sc, sh, w, wkt, wgu, wgut, bg, bgt)


def _gla_chunk(qd, kdt, ket, dec, v, state, tril):
    att = jnp.where(tril, _dot(qd, kdt), 0.0)
    o = _dot(qd, state.astype(BF16)) + _dot(att.astype(BF16), v)
    return o, state * dec + _dot(ket, v)


def _gla_prompt_kernel(qd_ref, kdt_ref, ket_ref, dec_ref, v_ref, o_ref, s_ref, state_sc, *, chunk):
    g = pl.program_id(2)

    @pl.when(g == 0)
    def _():
        state_sc[...] = jnp.zeros(state_sc.shape, F32)

    i = lax.broadcasted_iota(jnp.int32, (chunk, chunk), 0)
    j = lax.broadcasted_iota(jnp.int32, (chunk, chunk), 1)
    tril = j <= i
    state = state_sc[...]
    group = qd_ref.shape[0]
    for c in range(group // chunk):
        sl = slice(c * chunk, (c + 1) * chunk)
        o, state = _gla_chunk(qd_ref[sl, :], kdt_ref[:, sl], ket_ref[:, sl],
                              dec_ref[:, c * chunk:c * chunk + 1], v_ref[sl, :], state, tril)
        o_ref[sl, :] = o
    state_sc[...] = state

    @pl.when(g == pl.num_programs(2) - 1)
    def _():
        s_ref[0, 0] = state


def _gla_prompt(qd, kdt, ket, dec, v, batch, seq, chunk):
    n, hk = qd.shape
    hv = v.shape[1]
    dk, dv = hk // GLA_HEADS, hv // GLA_HEADS
    group = min(LANES, seq)
    ng = seq // group
    row_k = pl.BlockSpec((group, dk), lambda b, h, g: (b * ng + g, h))
    row_v = pl.BlockSpec((group, dv), lambda b, h, g: (b * ng + g, h))
    col_k = pl.BlockSpec((dk, group), lambda b, h, g: (h, b * ng + g))
    return pl.pallas_call(
        functools.partial(_gla_prompt_kernel, chunk=chunk),
        grid=(batch, GLA_HEADS, ng),
        in_specs=[row_k, col_k, col_k, col_k, row_v],
        out_specs=[row_v, pl.BlockSpec((1, 1, dk, dv), lambda b, h, g: (b, h, 0, 0))],
        out_shape=[jax.ShapeDtypeStruct((n, hv), F32),
                   jax.ShapeDtypeStruct((batch, GLA_HEADS, dk, dv), F32)],
        scratch_shapes=[pltpu.VMEM((dk, dv), F32)],
        compiler_params=_params("arbitrary", "arbitrary", "arbitrary"),
        name="gla_prompt",
    )(qd, kdt, ket, dec, v)


def _gla_sample_kernel(qd_ref, kdt_ref, ket_ref, dec_ref, v_ref, s0_ref, o_ref, s_ref, *, chunk):
    i = lax.broadcasted_iota(jnp.int32, (chunk, chunk), 0)
    j = lax.broadcasted_iota(jnp.int32, (chunk, chunk), 1)
    tril = j <= i
    group = qd_ref.shape[0]
    for c in range(group // chunk):
        sl = slice(c * chunk, (c + 1) * chunk)
        o, state = _gla_chunk(qd_ref[sl, :].astype(BF16), kdt_ref[:, sl].astype(BF16),
                              ket_ref[:, sl].astype(BF16), dec_ref[:, c * chunk:c * chunk + 1],
                              v_ref[sl, :].astype(BF16), s0_ref[c, 0], tril)
        o_ref[sl, :] = o
        s_ref[c, 0] = state


def _gla_sample(qd, kdt, ket, dec, v, s0, chunk):
    n, hk = qd.shape
    hv = v.shape[1]
    dk, dv = hk // GLA_HEADS, hv // GLA_HEADS
    group = min(LANES, n)
    seqs = group // chunk
    row_k = pl.BlockSpec((group, dk), lambda g, h: (g, h))
    row_v = pl.BlockSpec((group, dv), lambda g, h: (g, h))
    col_k = pl.BlockSpec((dk, group), lambda g, h: (h, g))
    st = pl.BlockSpec((seqs, 1, dk, dv), lambda g, h: (g, h, 0, 0))
    return pl.pallas_call(
        functools.partial(_gla_sample_kernel, chunk=chunk),
        grid=(n // group, GLA_HEADS),
        in_specs=[row_k, col_k, col_k, col_k, row_v, st],
        out_specs=[row_v, st],
        out_shape=[jax.ShapeDtypeStruct((n, hv), F32),
                   jax.ShapeDtypeStruct(s0.shape, F32)],
        compiler_params=_params("arbitrary", "arbitrary"),
        name="gla_sample",
    )(qd.astype(F32), kdt.astype(F32), ket.astype(F32), dec, v.astype(F32), s0)


def _even_weights(w_in, w_q_up, w_kv_up, q_lora, kv_lora):
    d = w_in.shape[0]
    sb_w = SB_HEADS * SB_DIM
    o = 0
    q_a = w_in[:, o:o + q_lora]; o += q_lora
    kv_a = w_in[:, o:o + kv_lora]; o += kv_lora
    k_r = w_in[:, o:o + MLA_ROPE]; o += MLA_ROPE
    sb = w_in[:, o:o + 3 * sb_w]
    w1 = jnp.concatenate([q_a, kv_a, sb, k_r, jnp.zeros((d, LANES - MLA_ROPE), w_in.dtype)], axis=1)

    qh = w_q_up.reshape(q_lora, MLA_HEADS, MLA_NOPE + MLA_ROPE)
    wq = jnp.concatenate([
        qh[:, :, :MLA_NOPE].reshape(q_lora, -1),
        qh[:, :, MLA_NOPE:MLA_NOPE + ROPE_HALF].reshape(q_lora, -1),
        qh[:, :, MLA_NOPE + ROPE_HALF:].reshape(q_lora, -1)], axis=1)

    eye_h = jnp.eye(MLA_HEADS, dtype=w_kv_up.dtype)
    eye_r = jnp.eye(ROPE_HALF, dtype=w_kv_up.dtype)
    k_up = jnp.transpose(w_kv_up[:, :, :MLA_NOPE], (1, 2, 0))
    nope_rows = (k_up[:, :, None, :] * eye_h[:, None, :, None]).reshape(MLA_HEADS * MLA_NOPE, MLA_HEADS, kv_lora)
    rope_rows = (eye_h[:, None, :, None] * eye_r[None, :, None, :]).reshape(MLA_HEADS * ROPE_HALF, MLA_HEADS, ROPE_HALF)
    pad = MLA_QPAD - kv_lora - MLA_ROPE

    def z(rows, cols):
        return jnp.zeros((rows, MLA_HEADS, cols), w_kv_up.dtype)

    n_r = MLA_HEADS * ROPE_HALF
    wcat = jnp.concatenate([
        jnp.concatenate([nope_rows, z(nope_rows.shape[0], MLA_QPAD - kv_lora)], axis=2),
        jnp.concatenate([z(n_r, kv_lora), rope_rows, z(n_r, ROPE_HALF + pad)], axis=2),
        jnp.concatenate([z(n_r, kv_lora + ROPE_HALF), rope_rows, z(n_r, pad)], axis=2)], axis=0)
    wcat = wcat.reshape(-1, MLA_HEADS * MLA_QPAD)

    v_up = jnp.transpose(w_kv_up[:, :, MLA_NOPE:], (1, 0, 2))
    bdv = (v_up[:, :, None, :] * eye_h[:, None, :, None]).reshape(MLA_HEADS * kv_lora, MLA_HEADS * MLA_V)
    return w1.astype(BF16), wq.astype(BF16), wcat.astype(BF16), bdv.astype(BF16)


def _tokens_minor(pool):
    feat = 1
    for s in pool.shape[3:]:
        feat *= s
    return jnp.moveaxis(pool, 2, -1).reshape(pool.shape[:2] + (feat, pool.shape[2]))


def _rope_tables(pos):
    inv_freq = ROPE_THETA ** (-jnp.arange(ROPE_HALF, dtype=F32) / ROPE_HALF)
    ang = pos.astype(F32)[:, None] * inv_freq[None, :]
    cos, sin = jnp.cos(ang), jnp.sin(ang)
    t = pos.shape[0]
    z16 = jnp.zeros((t, ROPE_HALF), F32)
    zpad = jnp.zeros((t, LANES - MLA_ROPE), F32)
    return (jnp.tile(cos, (1, MLA_HEADS)), jnp.tile(sin, (1, MLA_HEADS)),
            jnp.concatenate([cos, cos, zpad], axis=1),
            jnp.concatenate([-sin, z16, zpad], axis=1),
            jnp.concatenate([z16, sin, zpad], axis=1))


def _odd_weights(w_in, w_gate_up, b_gate):
    d = w_in.shape[0]
    hk = w_gate_up.shape[1]
    rank = w_gate_up.shape[0]
    hv = (w_in.shape[1] - 2 * hk - rank) // 2
    q = w_in[:, :hk]
    k = w_in[:, hk:2 * hk]
    rest = w_in[:, 2 * hk:]
    w = jnp.concatenate([q, rest, jnp.zeros((d, LANES - rank), w_in.dtype)], axis=1)
    wgu = jnp.concatenate([w_gate_up, jnp.zeros((LANES - rank, hk), w_gate_up.dtype)], axis=0)
    del hv
    return (w.astype(BF16), k.T.astype(BF16), wgu.astype(BF16), wgu.T.astype(BF16),
            b_gate.reshape(1, hk), b_gate.reshape(hk, 1))


def kernel(x_prompt, x_sample, c_prompt, c_sample, cache_mla_ckv, cache_mla_krope, cache_sb_k, cache_sb_v,
           state_gla, page_table, w_in_even, q_norm_g, w_q_up, kv_norm_g, w_kv_up, w_out_even,
           w_in_odd, w_gate_up, b_gate, gla_norm_g, w_out_odd, ada_w, ada_b, ln_g, ln_b,
           router_w, router_b, w_exp_gate, w_exp_up, w_exp_down):
    batch, seq, d = x_prompt.shape
    dec_batch, dec, _ = x_sample.shape
    depth = ada_w.shape[0]
    page = cache_mla_ckv.shape[2]
    past_len = page_table.shape[1] * page
    alpha = (2.0 * depth) ** 0.25
    n_p, n_s = batch * seq, dec_batch * dec
    q_lora, kv_lora = q_norm_g.shape[1], kv_norm_g.shape[1]
    n_exp = router_w.shape[1]
    per_group = n_exp // N_GROUPS

    c_rows = batch + dec_batch
    c_pad = -c_rows % 8
    c_all = jnp.concatenate([c_prompt, c_sample, jnp.zeros((c_pad, d), F32)], axis=0)
    mod = _ada(c_all, ada_w, ada_b)

    def mods(l):
        mp = [m.reshape(batch, 1, d) for m in jnp.split(mod[l, :batch], 6, axis=-1)]
        ms = jnp.split(jnp.repeat(mod[l, batch:c_rows], dec, axis=0), 6, axis=-1)
        return mp, ms

    perm = jnp.arange(n_exp).reshape(N_GROUPS, per_group).T.reshape(-1)
    rw = router_w.T[perm]
    rb = router_b[perm].reshape(n_exp, 1)

    pos_p = jnp.arange(seq, dtype=jnp.int32)
    pos_s = past_len + jnp.arange(dec, dtype=jnp.int32)
    tabs_p = _rope_tables(pos_p)
    tabs_s = tuple(jnp.tile(t, (dec_batch, 1)) for t in _rope_tables(pos_s))

    xp = x_prompt.reshape(n_p, d)
    xs = x_sample.reshape(n_s, d)
    outs = {k: [] for k in ("ckv_p", "ckv_s", "kr_p", "kr_s", "sbk_p", "sbk_s", "sbv_p", "sbv_s", "gla_p", "gla_s")}

    for l in range(depth):
        (sh1p, sc1p, g1p, sh2p, sc2p, g2p), (sh1s, sc1s, g1s, sh2s, sc2s, g2s) = mods(l)
        lng1, lnb1 = ln_g[l, 0].reshape(1, d), ln_b[l, 0].reshape(1, d)
        lng2, lnb2 = ln_g[l, 1].reshape(1, d), ln_b[l, 1].reshape(1, d)
        if l % 2 == 0:
            e = l // 2
            w1, wq, wcat, bdv = _even_weights(w_in_even[e], w_q_up[e], w_kv_up[e], q_lora, kv_lora)
            qg, kvg = q_norm_g[e].reshape(1, -1), kv_norm_g[e].reshape(1, -1)
            wo = w_out_even[e].astype(BF16)
            (qcat_p, kcat_p, ckv_p, kr_p, sbq_p, sbk_p, sbv_p, sbk16_p, sbv16_p) = _even_proj(
                xp, sc1p, sh1p, tabs_p, False, seq, w1, qg, wq, kvg, wcat)
            (qcat_s, kcat_s, ckv_s, kr_s, sbq_s, sbk_s, sbv_s, sbk16_s, sbv16_s) = _even_proj(
                xs, sc1s, sh1s, tabs_s, True, dec, w1, qg, wq, kvg, wcat)
            olat_p = _mla_prompt(qcat_p, kcat_p, batch, seq)
            osb_p = _sb_prompt(sbq_p, sbk16_p, sbv16_p, batch, seq)
            olat_s = _mla_sample(qcat_s, kcat_s, e, cache_mla_ckv, _tokens_minor(cache_mla_krope),
                                 page_table, dec)
            osb_s = _sb_sample(sbq_s, sbk16_s, sbv16_s, e, _tokens_minor(cache_sb_k),
                               _tokens_minor(cache_sb_v), page_table, dec)
            consts = (bdv, wo, lng1, lnb1, rw, rb)
            kern = functools.partial(_even_post_kernel, alpha=alpha)
            x1p, h2p, gates_p = _post_call(kern, "even_post", (olat_p, osb_p), xp, (g1p, sc2p, sh2p), consts, seq)
            x1s, h2s, gates_s = _post_call(kern, "even_post", (olat_s, osb_s), xs, (g1s, sc2s, sh2s), consts, dec)
            outs["ckv_p"].append(ckv_p.reshape(batch, seq, kv_lora))
            outs["ckv_s"].append(ckv_s.reshape(dec_batch, dec, kv_lora))
            outs["kr_p"].append(kr_p.reshape(batch, seq, MLA_ROPE))
            outs["kr_s"].append(kr_s.reshape(dec_batch, dec, MLA_ROPE))
            outs["sbk_p"].append(sbk_p.reshape(batch, seq, SB_HEADS, SB_DIM))
            outs["sbk_s"].append(sbk_s.reshape(dec_batch, dec, SB_HEADS, SB_DIM))
            outs["sbv_p"].append(sbv_p.reshape(batch, seq, SB_HEADS, SB_DIM))
            outs["sbv_s"].append(sbv_s.reshape(dec_batch, dec, SB_HEADS, SB_DIM))
        else:
            o = l // 2
            w, wkt, wgu, wgut, bg, bgt = _odd_weights(w_in_odd[o], w_gate_up[o], b_gate[o])
            hv = gla_norm_g.shape[1] * gla_norm_g.shape[2]
            ng = gla_norm_g[o].reshape(1, hv)
            wo = w_out_odd[o].astype(BF16)
            chunk_p, chunk_s = min(GLA_CHUNK, seq), min(GLA_CHUNK, dec)
            qd, kdt, ket, dcy, v, r_p = _odd_proj(xp, sc1p, sh1p, seq, w, wkt, wgu, wgut, bg, bgt, chunk_p)
            o_p, st_p = _gla_prompt(qd, kdt, ket, dcy, v, batch, seq, chunk_p)
            qd, kdt, ket, dcy, v, r_s = _odd_proj(xs, sc1s, sh1s, dec, w, wkt, wgu, wgut, bg, bgt, chunk_s)
            o_s, st_s = _gla_sample(qd, kdt, ket, dcy, v, state_gla[o], chunk_s)
            consts = (ng, wo, lng1, lnb1, rw, rb)
            kern = functools.partial(_odd_post_kernel, alpha=alpha, dv=gla_norm_g.shape[2])
            x1p, h2p, gates_p = _post_call(kern, "odd_post", (o_p, r_p), xp, (g1p, sc2p, sh2p), consts, seq)
            x1s, h2s, gates_s = _post_call(kern, "odd_post", (o_s, r_s), xs, (g1s, sc2s, sh2s), consts, dec)
            outs["gla_p"].append(st_p)
            outs["gla_s"].append(st_s)
        wg, wu, wd = w_exp_gate[l].astype(BF16), w_exp_up[l].astype(BF16), w_exp_down[l].astype(BF16)
        xp = _moe(h2p, gates_p, x1p, g2p, wg, wu, wd, lng2, lnb2, seq, alpha)
        xs = _moe(h2s, gates_s, x1s, g2s, wg, wu, wd, lng2, lnb2, dec, alpha)

    return (xp.reshape(batch, seq, d), xs.reshape(dec_batch, dec, d),
            jnp.stack(outs["ckv_p"]), jnp.stack(outs["ckv_s"]),
            jnp.stack(outs["kr_p"]), jnp.stack(outs["kr_s"]),
            jnp.stack(outs["sbk_p"]), jnp.stack(outs["sbk_s"]),
            jnp.stack(outs["sbv_p"]), jnp.stack(outs["sbv_s"]),
            jnp.stack(outs["gla_p"]), jnp.stack(outs["gla_s"]))
```

```python
import functools

import jax
import jax.numpy as jnp
from jax import lax
from jax.experimental import pallas as pl
from jax.experimental.pallas import tpu as pltpu

F32 = jnp.float32
BF16 = jnp.bfloat16

MLA_HEADS = 8
MLA_NOPE = 64
MLA_ROPE = 32
MLA_V = 64
ROPE_THETA = 10000.0
SB_HEADS = 8
SB_DIM = 64
GLA_HEADS = 4
GLA_TAU = 16.0
GLA_CHUNK = 32
N_EXPERTS = 16
N_GROUPS = 4
PER_GROUP = N_EXPERTS // N_GROUPS
NORM_EPS = 1e-6
LN_EPS = 1e-5

LANES = 128
ROPE_HALF = MLA_ROPE // 2
MLA_QPAD = 384
ROW_TILE = 256
MOE_ROW_TILE = 1024
ATT_TILE = 512
SB_TILE = 256
PAGES_PER_STEP = 32
VMEM_LIMIT = 56 * 1024 * 1024


def _params(*sem):
    return pltpu.CompilerParams(dimension_semantics=sem, vmem_limit_bytes=VMEM_LIMIT)


def _dot(a, b):
    return jnp.dot(a, b, preferred_element_type=F32)


def _dot_nt(a, b):
    return lax.dot_general(a, b, (((1,), (1,)), ((), ())), preferred_element_type=F32)


def _split_dot(x, m):
    hi = x.astype(BF16)
    lo = (x - hi.astype(F32)).astype(BF16)
    return _dot(hi, m) + _dot(lo, m)


def _sigmoid(x):
    return 1.0 / (1.0 + jnp.exp(-x))


def _silu(x):
    return x * _sigmoid(x)


def _rms(x, g):
    return x * lax.rsqrt(jnp.mean(x * x, axis=-1, keepdims=True) + NORM_EPS) * g


def _layer_norm(x, g, b):
    mu = jnp.mean(x, axis=-1, keepdims=True)
    xc = x - mu
    var = jnp.mean(xc * xc, axis=-1, keepdims=True)
    return xc * lax.rsqrt(var + LN_EPS) * g + b


def _ada_kernel(c_ref, w_ref, b_ref, o_ref):
    c = c_ref[...]
    o_ref[0] = _dot(_silu(c).astype(BF16), w_ref[0].astype(BF16)) + b_ref[0]


def _ada(c_all, ada_w, ada_b):
    depth, d, n6 = ada_w.shape
    rows = c_all.shape[0]
    tn = 1024
    return pl.pallas_call(
        _ada_kernel,
        grid=(depth, n6 // tn),
        in_specs=[
            pl.BlockSpec((rows, d), lambda l, n: (0, 0)),
            pl.BlockSpec((1, d, tn), lambda l, n: (l, 0, n)),
            pl.BlockSpec((1, 1, tn), lambda l, n: (l, 0, n)),
        ],
        out_specs=pl.BlockSpec((1, rows, tn), lambda l, n: (l, 0, n)),
        out_shape=jax.ShapeDtypeStruct((depth, rows, n6), F32),
        compiler_params=_params("arbitrary", "arbitrary"),
        name="ada",
    )(c_all, ada_w, ada_b.reshape(depth, 1, n6))


def _tile_rows(tile, n, seq_len, mod):
    return min(tile, seq_len if mod.ndim == 3 else n)


def _row_spec(r, c):
    return pl.BlockSpec((r, c), lambda i: (i, 0))


def _const_spec(shape):
    nd = len(shape)
    return pl.BlockSpec(shape, lambda i: (0,) * nd)


def _mod_spec(m, r, tiles_per_seq):
    if m.ndim == 3:
        return pl.BlockSpec((None, 1, m.shape[-1]), lambda i: (i // tiles_per_seq, 0, 0))
    return _row_spec(r, m.shape[-1])


def _tab_spec(t, r, tiles_per_seq, per_token):
    if per_token:
        return _row_spec(r, t.shape[-1])
    return pl.BlockSpec((r, t.shape[-1]), lambda i: (i % tiles_per_seq, 0))


def _even_proj_kernel(x_ref, sc_ref, sh_ref, cq_ref, sq_ref, ck_ref, skn_ref, skp_ref,
                      w1_ref, qg_ref, wq_ref, kvg_ref, wcat_ref,
                      qcat_ref, kcat_ref, ckv_ref, kr_ref, sbq_ref, sbk_ref, sbv_ref,
                      sbk16_ref, sbv16_ref, *, q_lora, kv_lora, sb_w):
    h = x_ref[...] * (1.0 + sc_ref[...]) + sh_ref[...]
    y = _dot(h.astype(BF16), w1_ref[...])
    o = 0
    q_a = y[:, o:o + q_lora]; o += q_lora
    kv_a = y[:, o:o + kv_lora]; o += kv_lora
    sbq = y[:, o:o + sb_w]; o += sb_w
    sbk = y[:, o:o + sb_w]; o += sb_w
    sbv = y[:, o:o + sb_w]; o += sb_w
    kr = y[:, o:o + LANES]

    q = _dot(_rms(q_a, qg_ref[...]).astype(BF16), wq_ref[...])
    n_nope = MLA_HEADS * MLA_NOPE
    x1 = q[:, n_nope:n_nope + LANES]
    x2 = q[:, n_nope + LANES:n_nope + 2 * LANES]
    cos, sin = cq_ref[...], sq_ref[...]
    qin = jnp.concatenate([q[:, :n_nope], x1 * cos - x2 * sin, x2 * cos + x1 * sin], axis=1)
    qcat_ref[...] = _dot(qin.astype(BF16), wcat_ref[...]).astype(BF16)

    ckv = _rms(kv_a, kvg_ref[...])
    ckv_ref[...] = ckv
    kro = (kr * ck_ref[...] + pltpu.roll(kr, LANES - ROPE_HALF, 1) * skn_ref[...]
           + pltpu.roll(kr, ROPE_HALF, 1) * skp_ref[...])
    kr_ref[...] = kro[:, :MLA_ROPE]
    kcat_ref[...] = jnp.concatenate([ckv, kro], axis=1).astype(BF16)

    sbq_ref[...] = sbq.astype(BF16)
    sbk_ref[...] = sbk
    sbv_ref[...] = sbv
    sbk16_ref[...] = sbk.astype(BF16)
    sbv16_ref[...] = sbv.astype(BF16)


def _even_proj(x, sc, sh, tabs, per_token, seq_len, w1, qg, wq, kvg, wcat):
    n, d = x.shape
    r = _tile_rows(ROW_TILE, n, seq_len, sc)
    tps = max(seq_len // r, 1)
    q_lora, kv_lora = qg.shape[-1], kvg.shape[-1]
    sb_w = SB_HEADS * SB_DIM
    qw = MLA_HEADS * MLA_QPAD
    kern = functools.partial(_even_proj_kernel, q_lora=q_lora, kv_lora=kv_lora, sb_w=sb_w)
    out_shapes = [
        jax.ShapeDtypeStruct((n, qw), BF16),
        jax.ShapeDtypeStruct((n, kv_lora + LANES), BF16),
        jax.ShapeDtypeStruct((n, kv_lora), F32),
        jax.ShapeDtypeStruct((n, MLA_ROPE), F32),
        jax.ShapeDtypeStruct((n, sb_w), BF16),
        jax.ShapeDtypeStruct((n, sb_w), F32),
        jax.ShapeDtypeStruct((n, sb_w), F32),
        jax.ShapeDtypeStruct((n, sb_w), BF16),
        jax.ShapeDtypeStruct((n, sb_w), BF16),
    ]
    return pl.pallas_call(
        kern,
        grid=(n // r,),
        in_specs=[_row_spec(r, d), _mod_spec(sc, r, tps), _mod_spec(sh, r, tps)]
        + [_tab_spec(t, r, tps, per_token) for t in tabs]
        + [_const_spec(w1.shape), _const_spec(qg.shape), _const_spec(wq.shape),
           _const_spec(kvg.shape), _const_spec(wcat.shape)],
        out_specs=[_row_spec(r, s.shape[1]) for s in out_shapes],
        out_shape=out_shapes,
        compiler_params=_params("arbitrary"),
        name="even_proj",
    )(x, sc, sh, *tabs, w1, qg, wq, kvg, wcat)


def _mla_prompt_kernel(q_ref, k_ref, o_ref, m_sc, l_sc, acc_sc, *, tile, kv_lora, scale):
    qi = pl.program_id(2)
    q = q_ref[...]
    m_sc[...] = jnp.full(m_sc.shape, -jnp.inf, F32)
    l_sc[...] = jnp.zeros(l_sc.shape, F32)
    acc_sc[...] = jnp.zeros(acc_sc.shape, F32)

    def keys(kb):
        return k_ref[pl.ds(pl.multiple_of(kb * tile, tile), tile), :]

    def scores(kb):
        return _dot_nt(q, keys(kb))

    def absorb(s, kb, masked):
        s = s * scale
        if masked:
            row = lax.broadcasted_iota(jnp.int32, s.shape, 0)
            col = lax.broadcasted_iota(jnp.int32, s.shape, 1)
            s = jnp.where(col <= row, s, -jnp.inf)
        m_prev = m_sc[...]
        m_new = jnp.maximum(m_prev, jnp.max(s, axis=1, keepdims=True))
        alpha = jnp.exp(m_prev - m_new)
        p = jnp.exp(s - m_new)
        l_sc[...] = alpha * l_sc[...] + jnp.sum(p, axis=1, keepdims=True)
        acc_sc[...] = alpha * acc_sc[...] + _dot(p.astype(BF16), keys(kb)[:, :kv_lora])
        m_sc[...] = m_new

    def body(kb, s):
        s_next = scores(kb + 1)
        absorb(s, kb, False)
        return s_next

    s = lax.fori_loop(0, qi, body, scores(0))
    absorb(s, qi, True)
    o_ref[...] = (acc_sc[...] / l_sc[...]).astype(BF16)


def _mla_prompt(qcat, kcat, batch, seq):
    n = qcat.shape[0]
    kv_lora = kcat.shape[1] - LANES
    tile = min(ATT_TILE, seq)
    nq = seq // tile
    scale = (MLA_NOPE + MLA_ROPE) ** -0.5
    kern = functools.partial(_mla_prompt_kernel, tile=tile, kv_lora=kv_lora, scale=scale)
    return pl.pallas_call(
        kern,
        grid=(batch, MLA_HEADS, nq),
        in_specs=[
            pl.BlockSpec((tile, MLA_QPAD), lambda b, h, i: (b * nq + i, h)),
            pl.BlockSpec((seq, kcat.shape[1]), lambda b, h, i: (b, 0)),
        ],
        out_specs=pl.BlockSpec((tile, kv_lora), lambda b, h, i: (b * nq + i, h)),
        out_shape=jax.ShapeDtypeStruct((n, MLA_HEADS * kv_lora), BF16),
        scratch_shapes=[pltpu.VMEM((tile, 1), F32), pltpu.VMEM((tile, 1), F32),
                        pltpu.VMEM((tile, kv_lora), F32)],
        compiler_params=_params("arbitrary", "arbitrary", "arbitrary"),
        name="mla_prompt",
    )(qcat, kcat)


def _suffix_matrix(n):
    row = lax.broadcasted_iota(jnp.int32, (n, n), 0)
    col = lax.broadcasted_iota(jnp.int32, (n, n), 1)
    return jnp.where(row > col, 1.0, 0.0).astype(BF16)


def _sb_log_terms(z, valid):
    t = jnp.log(1.0 + jnp.exp(-jnp.abs(z)))
    log_beta = jnp.minimum(z, 0.0) - t
    log_stay = log_beta - z
    if valid is not None:
        log_stay = jnp.where(valid, log_stay, 0.0)
    return log_beta, log_stay


def _sb_weights(log_beta, within, carry, valid):
    w = jnp.exp(log_beta + (carry + within))
    if valid is not None:
        w = jnp.where(valid, w, 0.0)
    return w


def _sb_prompt_kernel(q_ref, k_ref, v_ref, o_ref, *, tq, tk, scale):
    qi = pl.program_id(2)
    suffix = _suffix_matrix(tk)
    q_pos = qi * tq + lax.broadcasted_iota(jnp.int32, (tq, tk), 0)
    col = lax.broadcasted_iota(jnp.int32, (tq, tk), 1)
    heads = LANES // SB_DIM
    sls = [slice(hh * SB_DIM, (hh + 1) * SB_DIM) for hh in range(heads)]
    qs = [(q_ref[:, sl].astype(F32) * scale).astype(BF16) for sl in sls]

    def rows_of(kb):
        return pl.ds(pl.multiple_of(kb * tk, tk), tk)

    def scores(kb):
        k2 = k_ref[rows_of(kb), :]
        return tuple(_dot_nt(qs[hh], k2[:, sls[hh]]) for hh in range(heads))

    def absorb(z, kb, state, masked):
        v2 = v_ref[rows_of(kb), :]
        valid = (kb * tk + col) < q_pos if masked else None
        terms = []
        for hh in range(heads):
            log_beta, log_stay = _sb_log_terms(z[hh], valid)
            terms.append((log_beta, log_stay, _split_dot(log_stay, suffix)))
        new = []
        for hh in range(heads):
            carry, acc = state[hh]
            log_beta, log_stay, within = terms[hh]
            w = _sb_weights(log_beta, within, carry, valid)
            new.append((carry + jnp.sum(log_stay, axis=1, keepdims=True),
                        acc + _dot(w.astype(BF16), v2[:, sls[hh]])))
        return tuple(new)

    top = (qi + 1) * (tq // tk) - 1
    n_masked = tq // tk
    state = tuple((jnp.zeros((tq, 1), F32), jnp.zeros((tq, SB_DIM), F32)) for _ in range(heads))
    z = scores(top)
    for i in range(n_masked):
        z_next = scores(jnp.maximum(top - 1 - i, 0))
        state = absorb(z, top - i, state, True)
        z = z_next

    def body(i, c):
        kb = top - n_masked - i
        z_next = scores(jnp.maximum(kb - 1, 0))
        return z_next, absorb(c[0], kb, c[1], False)

    z, state = lax.fori_loop(0, top + 1 - n_masked, body, (z, state))
    o_ref[...] = jnp.concatenate([acc for _, acc in state], axis=1).astype(BF16)


def _sb_prompt(sbq, sbk, sbv, batch, seq):
    n, w = sbq.shape
    tk = min(SB_TILE, seq)
    tile = min(2 * tk, seq)
    nq = seq // tile
    kern = functools.partial(_sb_prompt_kernel, tq=tile, tk=tk, scale=SB_DIM ** -0.5)
    return pl.pallas_call(
        kern,
        grid=(batch, w // LANES, nq),
        in_specs=[
            pl.BlockSpec((tile, LANES), lambda b, h, i: (b * nq + i, h)),
            pl.BlockSpec((seq, LANES), lambda b, h, i: (b, h)),
            pl.BlockSpec((seq, LANES), lambda b, h, i: (b, h)),
        ],
        out_specs=pl.BlockSpec((tile, LANES), lambda b, h, i: (b * nq + i, h)),
        out_shape=jax.ShapeDtypeStruct((n, w), BF16),
        compiler_params=_params("arbitrary", "arbitrary", "arbitrary"),
        name="sb_prompt",
    )(sbq, sbk, sbv)


def _pad_rows(x, rows):
    return jnp.concatenate([x, jnp.zeros((rows - x.shape[0], x.shape[1]), x.dtype)], axis=0)


def _mla_sample_kernel(pt_ref, q_ref, kn_ref, *refs, n_pages, page, kv_lora, scale, dec):
    del pt_ref
    ckv_refs = refs[:n_pages]
    krt_refs = refs[n_pages:2 * n_pages]
    o_ref = refs[2 * n_pages]
    q_sc, kv_sc, krt_sc, m_sc, l_sc, acc_sc = refs[2 * n_pages + 1:]
    g = pl.program_id(1)
    rows = MLA_HEADS * dec

    def update(parts):
        m_prev = m_sc[...]
        m_new = m_prev
        for s, _ in parts:
            m_new = jnp.maximum(m_new, jnp.max(s, axis=1, keepdims=True))
        alpha = jnp.exp(m_prev - m_new)
        l_new = alpha * l_sc[...]
        acc = alpha * acc_sc[...]
        for s, values in parts:
            p = jnp.exp(s - m_new)
            l_new = l_new + jnp.sum(p, axis=1, keepdims=True)
            acc = acc + _dot(p.astype(BF16), values)
        l_sc[...] = l_new
        acc_sc[...] = acc
        m_sc[...] = m_new

    @pl.when(g == 0)
    def _():
        qf = q_ref[0].astype(F32)
        q = jnp.concatenate([qf[:, h * MLA_QPAD:(h + 1) * MLA_QPAD] for h in range(MLA_HEADS)], axis=0)
        q = q.astype(BF16)
        q_sc[...] = q
        m_sc[...] = jnp.full(m_sc.shape, -jnp.inf, F32)
        l_sc[...] = jnp.zeros(l_sc.shape, F32)
        acc_sc[...] = jnp.zeros(acc_sc.shape, F32)
        kn = _pad_rows(kn_ref[0].astype(F32), page).astype(BF16)
        s = _dot_nt(q, kn) * scale
        t = lax.broadcasted_iota(jnp.int32, (rows, page), 0) % dec
        j = lax.broadcasted_iota(jnp.int32, (rows, page), 1)
        update([(jnp.where(j <= t, s, -jnp.inf), kn[:, :kv_lora])])

    q = q_sc[...]
    half = n_pages // 2
    groups = [slice(0, half * page), slice(half * page, n_pages * page)] if half else [slice(0, n_pages * page)]
    parts = []
    for keys in groups:
        for p in range(keys.start // page, keys.stop // page):
            kv_sc[p * page:(p + 1) * page, :] = ckv_refs[p][...].astype(BF16)
            krt_sc[:, p * page:(p + 1) * page] = krt_refs[p][...].astype(BF16)
        kv = kv_sc[keys, :]
        s = _dot_nt(q[:, :kv_lora], kv) + _dot(q[:, kv_lora:kv_lora + MLA_ROPE], krt_sc[:, keys])
        parts.append((s * scale, kv))
    update(parts)

    @pl.when(g == pl.num_programs(1) - 1)
    def _():
        o = acc_sc[...] / l_sc[...]
        o_ref[0] = jnp.concatenate([o[h * dec:(h + 1) * dec] for h in range(MLA_HEADS)], axis=1).astype(BF16)


def _page_specs(layer, n_pages, pages_total, block):
    specs = []
    for p in range(n_pages):
        def index(b, g, pt, p=p):
            return (layer, pt[b * pages_total + pages_total - 1 - (g * n_pages + p)], 0, 0)
        specs.append(pl.BlockSpec((None, None) + block, index))
    return specs


def _mla_sample(qcat, kcat_new, layer, pool_ckv, pool_krt, page_table, dec):
    n, qw = qcat.shape
    dec_batch, pages_total = page_table.shape
    page, kv_lora = pool_ckv.shape[2:]
    n_pages = min(PAGES_PER_STEP, pages_total)
    rows = MLA_HEADS * dec
    kern = functools.partial(_mla_sample_kernel, n_pages=n_pages, page=page, kv_lora=kv_lora,
                             scale=(MLA_NOPE + MLA_ROPE) ** -0.5, dec=dec)
    grid_spec = pltpu.PrefetchScalarGridSpec(
        num_scalar_prefetch=1,
        grid=(dec_batch, pages_total // n_pages),
        in_specs=[
            pl.BlockSpec((1, dec, qw), lambda b, g, pt: (b, 0, 0)),
            pl.BlockSpec((1, dec, kcat_new.shape[1]), lambda b, g, pt: (b, 0, 0)),
        ] + _page_specs(layer, n_pages, pages_total, (page, kv_lora))
        + _page_specs(layer, n_pages, pages_total, (MLA_ROPE, page)),
        out_specs=pl.BlockSpec((1, dec, MLA_HEADS * kv_lora), lambda b, g, pt: (b, 0, 0)),
        scratch_shapes=[pltpu.VMEM((rows, MLA_QPAD), BF16),
                        pltpu.VMEM((n_pages * page, kv_lora), BF16),
                        pltpu.VMEM((MLA_ROPE, n_pages * page), BF16),
                        pltpu.VMEM((rows, 1), F32), pltpu.VMEM((rows, 1), F32),
                        pltpu.VMEM((rows, kv_lora), F32)],
    )
    out = pl.pallas_call(
        kern,
        grid_spec=grid_spec,
        out_shape=jax.ShapeDtypeStruct((dec_batch, dec, MLA_HEADS * kv_lora), BF16),
        compiler_params=_params("arbitrary", "arbitrary"),
        name="mla_sample",
    )(page_table.reshape(-1), qcat.reshape(dec_batch, dec, qw),
      kcat_new.reshape(dec_batch, dec, -1), *([pool_ckv] * n_pages), *([pool_krt] * n_pages))
    return out.reshape(n, MLA_HEADS * kv_lora)


def _sb_sample_kernel(pt_ref, q_ref, kn_ref, vn_ref, *refs, n_pages, page, scale, dec):
    del pt_ref
    k_refs = refs[:n_pages]
    v_refs = refs[n_pages:2 * n_pages]
    o_ref = refs[2 * n_pages]
    q_sc, k_sc, v_sc, carry_sc, acc_sc = refs[2 * n_pages + 1:]
    g = pl.program_id(1)
    rows = SB_HEADS * dec
    width = SB_HEADS * SB_DIM
    suffix = _suffix_matrix(page)

    def head_mask(shape):
        r = lax.broadcasted_iota(jnp.int32, shape, 0)
        c = lax.broadcasted_iota(jnp.int32, shape, 1)
        return (r // dec) == (c // SB_DIM)

    def log_terms(z, valid, n):
        zr = jnp.concatenate([z[:, c * page:(c + 1) * page] for c in range(n)], axis=0)
        log_beta, log_stay = _sb_log_terms(zr, valid)
        return log_beta, log_stay, _split_dot(log_stay, suffix)

    def weights(terms, carry, valid, n):
        log_beta, log_stay, within = terms
        total = jnp.sum(log_stay, axis=1, keepdims=True)
        carries = []
        for c in range(n):
            carries.append(carry)
            carry = carry + total[c * rows:(c + 1) * rows]
        w = _sb_weights(log_beta, within, jnp.concatenate(carries, axis=0), valid)
        w = jnp.concatenate([w[c * rows:(c + 1) * rows] for c in range(n)], axis=1)
        return w.astype(BF16), carry

    @pl.when(g == 0)
    def _():
        qf = q_ref[0].astype(F32)
        q = jnp.concatenate([qf] * SB_HEADS, axis=0)
        q = jnp.where(head_mask((rows, width)), q, 0.0).astype(BF16)
        q_sc[...] = q
        kn = _pad_rows(kn_ref[0].astype(F32), page).astype(BF16)
        vn = _pad_rows(vn_ref[0].astype(F32), page).astype(BF16)
        t = lax.broadcasted_iota(jnp.int32, (rows, page), 0) % dec
        j = lax.broadcasted_iota(jnp.int32, (rows, page), 1)
        valid = j < t
        w, carry = weights(log_terms(_dot_nt(q, kn) * scale, valid, 1), jnp.zeros((rows, 1), F32), valid, 1)
        carry_sc[...] = carry
        acc_sc[...] = _dot(w, vn)

    q = q_sc[...]
    half = n_pages // 2
    groups = [slice(0, half * page), slice(half * page, n_pages * page)] if half else [slice(0, n_pages * page)]
    terms = []
    for gi, cols in enumerate(groups):
        for p in range(cols.start // page, cols.stop // page):
            k_sc[:, p * page:(p + 1) * page] = k_refs[p][...].astype(BF16)
            v_sc[:, p * page:(p + 1) * page] = v_refs[p][...].astype(BF16)
        terms.append(log_terms(_dot(q, k_sc[:, cols]) * scale, None, (cols.stop - cols.start) // page))
    carry = carry_sc[...]
    acc = acc_sc[...]
    for gi, cols in enumerate(groups):
        w, carry = weights(terms[gi], carry, None, (cols.stop - cols.start) // page)
        acc = acc + _dot_nt(w, v_sc[:, cols])
    carry_sc[...] = carry
    acc_sc[...] = acc

    @pl.when(g == pl.num_programs(1) - 1)
    def _():
        acc = jnp.where(head_mask((rows, width)), acc_sc[...], 0.0)
        o = acc[0:dec]
        for h in range(1, SB_HEADS):
            o = o + acc[h * dec:(h + 1) * dec]
        o_ref[0] = o.astype(BF16)


def _sb_sample(sbq, sbk_new, sbv_new, layer, pool_kt, pool_vt, page_table, dec):
    n, width = sbq.shape
    dec_batch, pages_total = page_table.shape
    page = pool_kt.shape[3]
    n_pages = min(PAGES_PER_STEP, pages_total)
    rows = SB_HEADS * dec
    kern = functools.partial(_sb_sample_kernel, n_pages=n_pages, page=page, scale=SB_DIM ** -0.5, dec=dec)
    new_spec = pl.BlockSpec((1, dec, width), lambda b, g, pt: (b, 0, 0))
    grid_spec = pltpu.PrefetchScalarGridSpec(
        num_scalar_prefetch=1,
        grid=(dec_batch, pages_total // n_pages),
        in_specs=[new_spec, new_spec, new_spec]
        + _page_specs(layer, n_pages, pages_total, (width, page))
        + _page_specs(layer, n_pages, pages_total, (width, page)),
        out_specs=new_spec,
        scratch_shapes=[pltpu.VMEM((rows, width), BF16),
                        pltpu.VMEM((width, n_pages * page), BF16),
                        pltpu.VMEM((width, n_pages * page), BF16),
                        pltpu.VMEM((rows, 1), F32),
                        pltpu.VMEM((rows, width), F32)],
    )
    out = pl.pallas_call(
        kern,
        grid_spec=grid_spec,
        out_shape=jax.ShapeDtypeStruct((dec_batch, dec, width), BF16),
        compiler_params=_params("arbitrary", "arbitrary"),
        name="sb_sample",
    )(page_table.reshape(-1), sbq.reshape(dec_batch, dec, width), sbk_new.reshape(dec_batch, dec, width),
      sbv_new.reshape(dec_batch, dec, width), *([pool_kt] * n_pages), *([pool_vt] * n_pages))
    return out.reshape(n, width)


def _router_gates(h2, rw_ref, rb_ref):
    logits = lax.dot_general(rw_ref[...], h2, (((1,), (1,)), ((), ())),
                             precision=lax.Precision.HIGHEST, preferred_element_type=F32)
    aff = _sigmoid(logits)
    sel = aff + rb_ref[...]
    per = PER_GROUP
    s = [sel[j * N_GROUPS:(j + 1) * N_GROUPS] for j in range(per)]
    a = [aff[j * N_GROUPS:(j + 1) * N_GROUPS] for j in range(per)]
    score = None
    for i in range(per):
        for j in range(i + 1, per):
            pair = s[i] + s[j]
            score = pair if score is None else jnp.maximum(score, pair)
    best = jnp.max(score, axis=0, keepdims=True)
    gidx = lax.broadcasted_iota(jnp.int32, score.shape, 0)
    first = jnp.min(jnp.where(score == best, gidx, N_GROUPS), axis=0, keepdims=True)
    in_group = gidx == first
    picked = []
    for j in range(per):
        rank = jnp.zeros(score.shape, F32)
        for k in range(per):
            if k == j:
                continue
            ahead = (s[k] >= s[j]) if k < j else (s[k] > s[j])
            rank = rank + jnp.where(ahead, 1.0, 0.0)
        picked.append(jnp.where(in_group, jnp.where(rank < 2.0, a[j], 0.0), 0.0))
    total = None
    for j in range(per):
        part = jnp.sum(picked[j], axis=0, keepdims=True)
        total = part if total is None else total + part
    gates_t = jnp.concatenate([p / total for p in picked], axis=0)
    gates_t = _pad_rows(gates_t, LANES)
    return gates_t.T


def _mix_tail(x, mix, g1, sc2, sh2, lng_ref, lnb_ref, rw_ref, rb_ref, alpha, x1_ref, h2_ref, gates_ref):
    x1 = _layer_norm(alpha * x + (1.0 + g1) * mix, lng_ref[...], lnb_ref[...])
    h2 = x1 * (1.0 + sc2) + sh2
    x1_ref[...] = x1
    h2_ref[...] = h2.astype(BF16)
    gates_ref[...] = _router_gates(h2, rw_ref, rb_ref)


def _even_post_kernel(olat_ref, osb_ref, x_ref, g1_ref, sc2_ref, sh2_ref, bdv_ref, wo_ref,
                      lng_ref, lnb_ref, rw_ref, rb_ref, x1_ref, h2_ref, gates_ref, *, alpha):
    o_mla = _dot(olat_ref[...], bdv_ref[...]).astype(BF16)
    nv = o_mla.shape[1]
    mix = _dot(o_mla, wo_ref[:nv, :]) + _dot(osb_ref[...], wo_ref[nv:, :])
    _mix_tail(x_ref[...], mix, g1_ref[...], sc2_ref[...], sh2_ref[...], lng_ref, lnb_ref,
              rw_ref, rb_ref, alpha, x1_ref, h2_ref, gates_ref)


def _odd_post_kernel(o_ref, r_ref, x_ref, g1_ref, sc2_ref, sh2_ref, ng_ref, wo_ref,
                     lng_ref, lnb_ref, rw_ref, rb_ref, x1_ref, h2_ref, gates_ref, *, alpha, dv):
    o = o_ref[...]
    parts = []
    for h in range(GLA_HEADS):
        sl = slice(h * dv, (h + 1) * dv)
        parts.append(_rms(o[:, sl], ng_ref[:, sl]))
    gated = jnp.concatenate(parts, axis=1) * _silu(r_ref[...])
    mix = _dot(gated.astype(BF16), wo_ref[...])
    _mix_tail(x_ref[...], mix, g1_ref[...], sc2_ref[...], sh2_ref[...], lng_ref, lnb_ref,
              rw_ref, rb_ref, alpha, x1_ref, h2_ref, gates_ref)


def _post_call(kern, name, acts, x, mods, consts, seq_len):
    n, d = x.shape
    r = _tile_rows(ROW_TILE, n, seq_len, mods[0])
    tps = max(seq_len // r, 1)
    out_shapes = [jax.ShapeDtypeStruct((n, d), F32), jax.ShapeDtypeStruct((n, d), BF16),
                  jax.ShapeDtypeStruct((n, LANES), F32)]
    return pl.pallas_call(
        kern,
        grid=(n // r,),
        in_specs=[_row_spec(r, a.shape[1]) for a in acts] + [_row_spec(r, d)]
        + [_mod_spec(m, r, tps) for m in mods] + [_const_spec(c.shape) for c in consts],
        out_specs=[_row_spec(r, s.shape[1]) for s in out_shapes],
        out_shape=out_shapes,
        compiler_params=_params("arbitrary"),
        name=name,
    )(*acts, x, *mods, *consts)


def _moe_kernel(h_ref, gates_ref, x1_ref, g2_ref, wg_ref, wu_ref, wd_ref, lng_ref, lnb_ref,
                o_ref, acc_sc, *, alpha):
    e = pl.program_id(1)

    @pl.when(e == 0)
    def _():
        acc_sc[...] = jnp.zeros(acc_sc.shape, F32)

    h = h_ref[...]
    a = _silu(_dot(h, wg_ref[0])) * _dot(h, wu_ref[0])
    y = _dot(a.astype(BF16), wd_ref[0])
    col = (e % PER_GROUP) * N_GROUPS + e // PER_GROUP
    gates = gates_ref[...]
    lane = lax.broadcasted_iota(jnp.int32, gates.shape, 1)
    gate = jnp.sum(jnp.where(lane == col, gates, 0.0), axis=1, keepdims=True)
    acc_sc[...] += gate * y

    @pl.when(e == pl.num_programs(1) - 1)
    def _():
        o_ref[...] = _layer_norm(alpha * x1_ref[...] + (1.0 + g2_ref[...]) * acc_sc[...],
                                 lng_ref[...], lnb_ref[...])


def _moe(h2, gates, x1, g2, wg, wu, wd, lng, lnb, seq_len, alpha):
    n, d = x1.shape
    r = _tile_rows(MOE_ROW_TILE, n, seq_len, g2)
    tps = max(seq_len // r, 1)
    n_exp, _, de = wg.shape
    if g2.ndim == 3:
        g2_spec = pl.BlockSpec((None, 1, d), lambda i, e: (i // tps, 0, 0))
    else:
        g2_spec = pl.BlockSpec((r, d), lambda i, e: (i, 0))
    return pl.pallas_call(
        functools.partial(_moe_kernel, alpha=alpha),
        grid=(n // r, n_exp),
        in_specs=[
            pl.BlockSpec((r, d), lambda i, e: (i, 0)),
            pl.BlockSpec((r, LANES), lambda i, e: (i, 0)),
            pl.BlockSpec((r, d), lambda i, e: (i, 0)),
            g2_spec,
            pl.BlockSpec((1, d, de), lambda i, e: (e, 0, 0)),
            pl.BlockSpec((1, d, de), lambda i, e: (e, 0, 0)),
            pl.BlockSpec((1, de, d), lambda i, e: (e, 0, 0)),
            pl.BlockSpec((1, d), lambda i, e: (0, 0)),
            pl.BlockSpec((1, d), lambda i, e: (0, 0)),
        ],
        out_specs=pl.BlockSpec((r, d), lambda i, e: (i, 0)),
        out_shape=jax.ShapeDtypeStruct((n, d), F32),
        scratch_shapes=[pltpu.VMEM((r, d), F32)],
        compiler_params=_params("arbitrary", "arbitrary"),
        name="moe",
    )(h2, gates, x1, g2, wg, wu, wd, lng, lnb)


def _odd_proj_kernel(x_ref, sc_ref, sh_ref, w_ref, wkt_ref, wgu_ref, wgut_ref, bg_ref, bgt_ref,
                     qd_ref, kdt_ref, ket_ref, dec_ref, v_ref, r_ref, *, hk, hv, chunk, qscale):
    h = (x_ref[...] * (1.0 + sc_ref[...]) + sh_ref[...]).astype(BF16)
    rows = h.shape[0]
    y = _dot(h, w_ref[...])
    q = y[:, :hk] * qscale
    v = y[:, hk:hk + hv]
    r = y[:, hk + hv:hk + 2 * hv]
    glr = y[:, hk + 2 * hv:hk + 2 * hv + LANES].astype(BF16)
    kt = _dot_nt(wkt_ref[...], h)

    def log_sigmoid(z):
        return jnp.minimum(z, 0.0) - jnp.log1p(jnp.exp(-jnp.abs(z)))

    log_a = log_sigmoid(_dot(glr, wgu_ref[...]) + bg_ref[...]) * (1.0 / GLA_TAU)
    log_at = log_sigmoid(_dot_nt(wgut_ref[...], glr) + bgt_ref[...]) * (1.0 / GLA_TAU)

    i = lax.broadcasted_iota(jnp.int32, (rows, rows), 0)
    j = lax.broadcasted_iota(jnp.int32, (rows, rows), 1)
    same = (i // chunk) == (j // chunk)
    lower = jnp.where(same & (j <= i), 1.0, 0.0).astype(BF16)
    upper = jnp.where(same & (i <= j), 1.0, 0.0).astype(BF16)
    block = jnp.where(same, 1.0, 0.0).astype(BF16)

    hi = log_a.astype(BF16)
    lo = (log_a - hi.astype(F32)).astype(BF16)
    g_cum = _dot(lower, hi) + _dot(lower, lo)
    g_cum_t = _split_dot(log_at, upper)
    g_tot_t = _split_dot(log_at, block)

    qd_ref[...] = (q * jnp.exp(g_cum)).astype(BF16)
    kdt_ref[...] = (kt * jnp.exp(-g_cum_t)).astype(BF16)
    ket_ref[...] = (kt * jnp.exp(g_tot_t - g_cum_t)).astype(BF16)
    dec_ref[...] = jnp.exp(g_tot_t)
    v_ref[...] = v.astype(BF16)
    r_ref[...] = r


def _odd_proj(x, sc, sh, seq_len, w, wkt, wgu, wgut, bg, bgt, chunk):
    n, d = x.shape
    r = _tile_rows(ROW_TILE, n, seq_len, sc)
    tps = max(seq_len // r, 1)
    hk = wkt.shape[0]
    hv = (w.shape[1] - hk - LANES) // 2
    kern = functools.partial(_odd_proj_kernel, hk=hk, hv=hv, chunk=chunk, qscale=(hk // GLA_HEADS) ** -0.5)
    col_spec = pl.BlockSpec((hk, r), lambda i: (0, i))
    out_shapes = [
        jax.ShapeDtypeStruct((n, hk), BF16),
        jax.ShapeDtypeStruct((hk, n), BF16),
        jax.ShapeDtypeStruct((hk, n), BF16),
        jax.ShapeDtypeStruct((hk, n), F32),
        jax.ShapeDtypeStruct((n, hv), BF16),
        jax.ShapeDtypeStruct((n, hv), F32),
    ]
    return pl.pallas_call(
        kern,
        grid=(n // r,),
        in_specs=[_row_spec(r, d), _mod_spec(sc, r, tps), _mod_spec(sh, r, tps)]
        + [_const_spec(c.shape) for c in (w, wkt, wgu, wgut, bg, bgt)],
        out_specs=[_row_spec(r, hk), col_spec, col_spec, col_spec, _row_spec(r, hv), _row_spec(r, hv)],
        out_shape=out_shapes,
        compiler_params=_params("arbitrary"),
        name="odd_proj",
    )(x, sc, sh, w, wkt, wgu, wgut, bg, bgt)


def _gla_chunk(qd, kdt, ket, dec, v, state, tril):
    att = jnp.where(tril, _dot(qd, kdt), 0.0)
    o = _dot(qd, state.astype(BF16)) + _dot(att.astype(BF16), v)
    return o, state * dec + _dot(ket, v)


def _gla_prompt_kernel(qd_ref, kdt_ref, ket_ref, dec_ref, v_ref, o_ref, s_ref, state_sc, *, chunk):
    g = pl.program_id(2)

    @pl.when(g == 0)
    def _():
        state_sc[...] = jnp.zeros(state_sc.shape, F32)

    i = lax.broadcasted_iota(jnp.int32, (chunk, chunk), 0)
    j = lax.broadcasted_iota(jnp.int32, (chunk, chunk), 1)
    tril = j <= i
    state = state_sc[...]
    group = qd_ref.shape[0]
    for c in range(group // chunk):
        sl = slice(c * chunk, (c + 1) * chunk)
        o, state = _gla_chunk(qd_ref[sl, :], kdt_ref[:, sl], ket_ref[:, sl],
                              dec_ref[:, c * chunk:c * chunk + 1], v_ref[sl, :], state, tril)
        o_ref[sl, :] = o
    state_sc[...] = state

    @pl.when(g == pl.num_programs(2) - 1)
    def _():
        s_ref[0, 0] = state


def _gla_prompt(qd, kdt, ket, dec, v, batch, seq, chunk):
    n, hk = qd.shape
    hv = v.shape[1]
    dk, dv = hk // GLA_HEADS, hv // GLA_HEADS
    group = min(LANES, seq)
    ng = seq // group
    row_k = pl.BlockSpec((group, dk), lambda b, h, g: (b * ng + g, h))
    row_v = pl.BlockSpec((group, dv), lambda b, h, g: (b * ng + g, h))
    col_k = pl.BlockSpec((dk, group), lambda b, h, g: (h, b * ng + g))
    return pl.pallas_call(
        functools.partial(_gla_prompt_kernel, chunk=chunk),
        grid=(batch, GLA_HEADS, ng),
        in_specs=[row_k, col_k, col_k, col_k, row_v],
        out_specs=[row_v, pl.BlockSpec((1, 1, dk, dv), lambda b, h, g: (b, h, 0, 0))],
        out_shape=[jax.ShapeDtypeStruct((n, hv), F32),
                   jax.ShapeDtypeStruct((batch, GLA_HEADS, dk, dv), F32)],
        scratch_shapes=[pltpu.VMEM((dk, dv), F32)],
        compiler_params=_params("arbitrary", "arbitrary", "arbitrary"),
        name="gla_prompt",
    )(qd, kdt, ket, dec, v)


def _gla_sample_kernel(qd_ref, kdt_ref, ket_ref, dec_ref, v_ref, s0_ref, o_ref, s_ref, *, chunk):
    i = lax.broadcasted_iota(jnp.int32, (chunk, chunk), 0)
    j = lax.broadcasted_iota(jnp.int32, (chunk, chunk), 1)
    tril = j <= i
    group = qd_ref.shape[0]
    for c in range(group // chunk):
        sl = slice(c * chunk, (c + 1) * chunk)
        o, state = _gla_chunk(qd_ref[sl, :].astype(BF16), kdt_ref[:, sl].astype(BF16),
                              ket_ref[:, sl].astype(BF16), dec_ref[:, c * chunk:c * chunk + 1],
                              v_ref[sl, :].astype(BF16), s0_ref[c, 0], tril)
        o_ref[sl, :] = o
        s_ref[c, 0] = state


def _gla_sample(qd, kdt, ket, dec, v, s0, chunk):
    n, hk = qd.shape
    hv = v.shape[1]
    dk, dv = hk // GLA_HEADS, hv // GLA_HEADS
    group = min(LANES, n)
    seqs = group // chunk
    row_k = pl.BlockSpec((group, dk), lambda g, h: (g, h))
    row_v = pl.BlockSpec((group, dv), lambda g, h: (g, h))
    col_k = pl.BlockSpec((dk, group), lambda g, h: (h, g))
    st = pl.BlockSpec((seqs, 1, dk, dv), lambda g, h: (g, h, 0, 0))
    return pl.pallas_call(
        functools.partial(_gla_sample_kernel, chunk=chunk),
        grid=(n // group, GLA_HEADS),
        in_specs=[row_k, col_k, col_k, col_k, row_v, st],
        out_specs=[row_v, st],
        out_shape=[jax.ShapeDtypeStruct((n, hv), F32),
                   jax.ShapeDtypeStruct(s0.shape, F32)],
        compiler_params=_params("arbitrary", "arbitrary"),
        name="gla_sample",
    )(qd.astype(F32), kdt.astype(F32), ket.astype(F32), dec, v.astype(F32), s0)


def _even_weights(w_in, w_q_up, w_kv_up, q_lora, kv_lora):
    d = w_in.shape[0]
    sb_w = SB_HEADS * SB_DIM
    o = 0
    q_a = w_in[:, o:o + q_lora]; o += q_lora
    kv_a = w_in[:, o:o + kv_lora]; o += kv_lora
    k_r = w_in[:, o:o + MLA_ROPE]; o += MLA_ROPE
    sb = w_in[:, o:o + 3 * sb_w]
    w1 = jnp.concatenate([q_a, kv_a, sb, k_r, jnp.zeros((d, LANES - MLA_ROPE), w_in.dtype)], axis=1)

    qh = w_q_up.reshape(q_lora, MLA_HEADS, MLA_NOPE + MLA_ROPE)
    wq = jnp.concatenate([
        qh[:, :, :MLA_NOPE].reshape(q_lora, -1),
        qh[:, :, MLA_NOPE:MLA_NOPE + ROPE_HALF].reshape(q_lora, -1),
        qh[:, :, MLA_NOPE + ROPE_HALF:].reshape(q_lora, -1)], axis=1)

    eye_h = jnp.eye(MLA_HEADS, dtype=w_kv_up.dtype)
    eye_r = jnp.eye(ROPE_HALF, dtype=w_kv_up.dtype)
    k_up = jnp.transpose(w_kv_up[:, :, :MLA_NOPE], (1, 2, 0))
    nope_rows = (k_up[:, :, None, :] * eye_h[:, None, :, None]).reshape(MLA_HEADS * MLA_NOPE, MLA_HEADS, kv_lora)
    rope_rows = (eye_h[:, None, :, None] * eye_r[None, :, None, :]).reshape(MLA_HEADS * ROPE_HALF, MLA_HEADS, ROPE_HALF)
    pad = MLA_QPAD - kv_lora - MLA_ROPE

    def z(rows, cols):
        return jnp.zeros((rows, MLA_HEADS, cols), w_kv_up.dtype)

    n_r = MLA_HEADS * ROPE_HALF
    wcat = jnp.concatenate([
        jnp.concatenate([nope_rows, z(nope_rows.shape[0], MLA_QPAD - kv_lora)], axis=2),
        jnp.concatenate([z(n_r, kv_lora), rope_rows, z(n_r, ROPE_HALF + pad)], axis=2),
        jnp.concatenate([z(n_r, kv_lora + ROPE_HALF), rope_rows, z(n_r, pad)], axis=2)], axis=0)
    wcat = wcat.reshape(-1, MLA_HEADS * MLA_QPAD)

    v_up = jnp.transpose(w_kv_up[:, :, MLA_NOPE:], (1, 0, 2))
    bdv = (v_up[:, :, None, :] * eye_h[:, None, :, None]).reshape(MLA_HEADS * kv_lora, MLA_HEADS * MLA_V)
    return w1.astype(BF16), wq.astype(BF16), wcat.astype(BF16), bdv.astype(BF16)


def _tokens_minor(pool):
    feat = 1
    for s in pool.shape[3:]:
        feat *= s
    return jnp.moveaxis(pool, 2, -1).reshape(pool.shape[:2] + (feat, pool.shape[2]))


def _rope_tables(pos):
    inv_freq = ROPE_THETA ** (-jnp.arange(ROPE_HALF, dtype=F32) / ROPE_HALF)
    ang = pos.astype(F32)[:, None] * inv_freq[None, :]
    cos, sin = jnp.cos(ang), jnp.sin(ang)
    t = pos.shape[0]
    z16 = jnp.zeros((t, ROPE_HALF), F32)
    zpad = jnp.zeros((t, LANES - MLA_ROPE), F32)
    return (jnp.tile(cos, (1, MLA_HEADS)), jnp.tile(sin, (1, MLA_HEADS)),
            jnp.concatenate([cos, cos, zpad], axis=1),
            jnp.concatenate([-sin, z16, zpad], axis=1),
            jnp.concatenate([z16, sin, zpad], axis=1))


def _odd_weights(w_in, w_gate_up, b_gate):
    d = w_in.shape[0]
    hk = w_gate_up.shape[1]
    rank = w_gate_up.shape[0]
    hv = (w_in.shape[1] - 2 * hk - rank) // 2
    q = w_in[:, :hk]
    k = w_in[:, hk:2 * hk]
    rest = w_in[:, 2 * hk:]
    w = jnp.concatenate([q, rest, jnp.zeros((d, LANES - rank), w_in.dtype)], axis=1)
    wgu = jnp.concatenate([w_gate_up, jnp.zeros((LANES - rank, hk), w_gate_up.dtype)], axis=0)
    del hv
    return (w.astype(BF16), k.T.astype(BF16), wgu.astype(BF16), wgu.T.astype(BF16),
            b_gate.reshape(1, hk), b_gate.reshape(hk, 1))


def kernel(x_prompt, x_sample, c_prompt, c_sample, cache_mla_ckv, cache_mla_krope, cache_sb_k, cache_sb_v,
           state_gla, page_table, w_in_even, q_norm_g, w_q_up, kv_norm_g, w_kv_up, w_out_even,
           w_in_odd, w_gate_up, b_gate, gla_norm_g, w_out_odd, ada_w, ada_b, ln_g, ln_b,
           router_w, router_b, w_exp_gate, w_exp_up, w_exp_down):
    batch, seq, d = x_prompt.shape
    dec_batch, dec, _ = x_sample.shape
    depth = ada_w.shape[0]
    page = cache_mla_ckv.shape[2]
    past_len = page_table.shape[1] * page
    alpha = (2.0 * depth) ** 0.25
    n_p, n_s = batch * seq, dec_batch * dec
    q_lora, kv_lora = q_norm_g.shape[1], kv_norm_g.shape[1]
    n_exp = router_w.shape[1]
    per_group = n_exp // N_GROUPS

    c_rows = batch + dec_batch
    c_pad = -c_rows % 8
    c_all = jnp.concatenate([c_prompt, c_sample, jnp.zeros((c_pad, d), F32)], axis=0)
    mod = _ada(c_all, ada_w, ada_b)

    def mods(l):
        mp = [m.reshape(batch, 1, d) for m in jnp.split(mod[l, :batch], 6, axis=-1)]
        ms = jnp.split(jnp.repeat(mod[l, batch:c_rows], dec, axis=0), 6, axis=-1)
        return mp, ms

    perm = jnp.arange(n_exp).reshape(N_GROUPS, per_group).T.reshape(-1)
    rw = router_w.T[perm]
    rb = router_b[perm].reshape(n_exp, 1)

    pos_p = jnp.arange(seq, dtype=jnp.int32)
    pos_s = past_len + jnp.arange(dec, dtype=jnp.int32)
    tabs_p = _rope_tables(pos_p)
    tabs_s = tuple(jnp.tile(t, (dec_batch, 1)) for t in _rope_tables(pos_s))

    xp = x_prompt.reshape(n_p, d)
    xs = x_sample.reshape(n_s, d)
    outs = {k: [] for k in ("ckv_p", "ckv_s", "kr_p", "kr_s", "sbk_p", "sbk_s", "sbv_p", "sbv_s", "gla_p", "gla_s")}

    for l in range(depth):
        (sh1p, sc1p, g1p, sh2p, sc2p, g2p), (sh1s, sc1s, g1s, sh2s, sc2s, g2s) = mods(l)
        lng1, lnb1 = ln_g[l, 0].reshape(1, d), ln_b[l, 0].reshape(1, d)
        lng2, lnb2 = ln_g[l, 1].reshape(1, d), ln_b[l, 1].reshape(1, d)
        if l % 2 == 0:
            e = l // 2
            w1, wq, wcat, bdv = _even_weights(w_in_even[e], w_q_up[e], w_kv_up[e], q_lora, kv_lora)
            qg, kvg = q_norm_g[e].reshape(1, -1), kv_norm_g[e].reshape(1, -1)
            wo = w_out_even[e].astype(BF16)
            (qcat_p, kcat_p, ckv_p, kr_p, sbq_p, sbk_p, sbv_p, sbk16_p, sbv16_p) = _even_proj(
                xp, sc1p, sh1p, tabs_p, False, seq, w1, qg, wq, kvg, wcat)
            (qcat_s, kcat_s, ckv_s, kr_s, sbq_s, sbk_s, sbv_s, sbk16_s, sbv16_s) = _even_proj(
                xs, sc1s, sh1s, tabs_s, True, dec, w1, qg, wq, kvg, wcat)
            olat_p = _mla_prompt(qcat_p, kcat_p, batch, seq)
            osb_p = _sb_prompt(sbq_p, sbk16_p, sbv16_p, batch, seq)
            olat_s = _mla_sample(qcat_s, kcat_s, e, cache_mla_ckv, _tokens_minor(cache_mla_krope),
                                 page_table, dec)
            osb_s = _sb_sample(sbq_s, sbk16_s, sbv16_s, e, _tokens_minor(cache_sb_k),
                               _tokens_minor(cache_sb_v), page_table, dec)
            consts = (bdv, wo, lng1, lnb1, rw, rb)
            kern = functools.partial(_even_post_kernel, alpha=alpha)
            x1p, h2p, gates_p = _post_call(kern, "even_post", (olat_p, osb_p), xp, (g1p, sc2p, sh2p), consts, seq)
            x1s, h2s, gates_s = _post_call(kern, "even_post", (olat_s, osb_s), xs, (g1s, sc2s, sh2s), consts, dec)
            outs["ckv_p"].append(ckv_p.reshape(batch, seq, kv_lora))
            outs["ckv_s"].append(ckv_s.reshape(dec_batch, dec, kv_lora))
            outs["kr_p"].append(kr_p.reshape(batch, seq, MLA_ROPE))
            outs["kr_s"].append(kr_s.reshape(dec_batch, dec, MLA_ROPE))
            outs["sbk_p"].append(sbk_p.reshape(batch, seq, SB_HEADS, SB_DIM))
            outs["sbk_s"].append(sbk_s.reshape(dec_batch, dec, SB_HEADS, SB_DIM))
            outs["sbv_p"].append(sbv_p.reshape(batch, seq, SB_HEADS, SB_DIM))
            outs["sbv_s"].append(sbv_s.reshape(dec_batch, dec, SB_HEADS, SB_DIM))
        else:
            o = l // 2
            w, wkt, wgu, wgut, bg, bgt = _odd_weights(w_in_odd[o], w_gate_up[o], b_gate[o])
            hv = gla_norm_g.shape[1] * gla_norm_g.shape[2]
            ng = gla_norm_g[o].reshape(1, hv)
            wo = w_out_odd[o].astype(BF16)
            chunk_p, chunk_s = min(GLA_CHUNK, seq), min(GLA_CHUNK, dec)
            qd, kdt, ket, dcy, v, r_p = _odd_proj(xp, sc1p, sh1p, seq, w, wkt, wgu, wgut, bg, bgt, chunk_p)
            o_p, st_p = _gla_prompt(qd, kdt, ket, dcy, v, batch, seq, chunk_p)
            qd, kdt, ket, dcy, v, r_s = _odd_proj(xs, sc1s, sh1s, dec, w, wkt, wgu, wgut, bg, bgt, chunk_s)
            o_s, st_s = _gla_sample(qd, kdt, ket, dcy, v, state_gla[o], chunk_s)
            consts = (ng, wo, lng1, lnb1, rw, rb)
            kern = functools.partial(_odd_post_kernel, alpha=alpha, dv=gla_norm_g.shape[2])
            x1p, h2p, gates_p = _post_call(kern, "odd_post", (o_p, r_p), xp, (g1p, sc2p, sh2p), consts, seq)
            x1s, h2s, gates_s = _post_call(kern, "odd_post", (o_s, r_s), xs, (g1s, sc2s, sh2s), consts, dec)
            outs["gla_p"].append(st_p)
            outs["gla_s"].append(st_s)
        wg, wu, wd = w_exp_gate[l].astype(BF16), w_exp_up[l].astype(BF16), w_exp_down[l].astype(BF16)
        xp = _moe(h2p, gates_p, x1p, g2p, wg, wu, wd, lng2, lnb2, seq, alpha)
        xs = _moe(h2s, gates_s, x1s, g2s, wg, wu, wd, lng2, lnb2, dec, alpha)

    return (xp.reshape(batch, seq, d), xs.reshape(dec_batch, dec, d),
            jnp.stack(outs["ckv_p"]), jnp.stack(outs["ckv_s"]),
            jnp.stack(outs["kr_p"]), jnp.stack(outs["kr_s"]),
            jnp.stack(outs["sbk_p"]), jnp.stack(outs["sbk_s"]),
            jnp.stack(outs["sbv_p"]), jnp.stack(outs["sbv_s"]),
            jnp.stack(outs["gla_p"]), jnp.stack(outs["gla_s"]))
```

```python
import functools

import jax
import jax.numpy as jnp
from jax import lax
from jax.experimental import pallas as pl
from jax.experimental.pallas import tpu as pltpu

F32 = jnp.float32
BF16 = jnp.bfloat16

MLA_HEADS = 8
MLA_NOPE = 64
MLA_ROPE = 32
MLA_V = 64
ROPE_THETA = 10000.0
SB_HEADS = 8
SB_DIM = 64
GLA_HEADS = 4
GLA_TAU = 16.0
GLA_CHUNK = 32
N_EXPERTS = 16
N_GROUPS = 4
PER_GROUP = N_EXPERTS // N_GROUPS
NORM_EPS = 1e-6
LN_EPS = 1e-5

LANES = 128
ROPE_HALF = MLA_ROPE // 2
MLA_QPAD = 384
ROW_TILE = 256
MOE_ROW_TILE = 1024
ATT_TILE = 512
SB_TILE = 256
PAGES_PER_STEP = 16
VMEM_LIMIT = 56 * 1024 * 1024


def _params(*sem):
    return pltpu.CompilerParams(dimension_semantics=sem, vmem_limit_bytes=VMEM_LIMIT)


def _dot(a, b):
    return jnp.dot(a, b, preferred_element_type=F32)


def _dot_nt(a, b):
    return lax.dot_general(a, b, (((1,), (1,)), ((), ())), preferred_element_type=F32)


def _split_dot(x, m):
    hi = x.astype(BF16)
    lo = (x - hi.astype(F32)).astype(BF16)
    return _dot(hi, m) + _dot(lo, m)


def _sigmoid(x):
    return 1.0 / (1.0 + jnp.exp(-x))


def _silu(x):
    return x * _sigmoid(x)


def _rms(x, g):
    return x * lax.rsqrt(jnp.mean(x * x, axis=-1, keepdims=True) + NORM_EPS) * g


def _layer_norm(x, g, b):
    mu = jnp.mean(x, axis=-1, keepdims=True)
    xc = x - mu
    var = jnp.mean(xc * xc, axis=-1, keepdims=True)
    return xc * lax.rsqrt(var + LN_EPS) * g + b


def _ada_kernel(c_ref, w_ref, b_ref, o_ref):
    c = c_ref[...]
    o_ref[0] = _dot(_silu(c).astype(BF16), w_ref[0].astype(BF16)) + b_ref[0]


def _ada(c_all, ada_w, ada_b):
    depth, d, n6 = ada_w.shape
    rows = c_all.shape[0]
    tn = 1024
    return pl.pallas_call(
        _ada_kernel,
        grid=(depth, n6 // tn),
        in_specs=[
            pl.BlockSpec((rows, d), lambda l, n: (0, 0)),
            pl.BlockSpec((1, d, tn), lambda l, n: (l, 0, n)),
            pl.BlockSpec((1, 1, tn), lambda l, n: (l, 0, n)),
        ],
        out_specs=pl.BlockSpec((1, rows, tn), lambda l, n: (l, 0, n)),
        out_shape=jax.ShapeDtypeStruct((depth, rows, n6), F32),
        compiler_params=_params("arbitrary", "arbitrary"),
        name="ada",
    )(c_all, ada_w, ada_b.reshape(depth, 1, n6))


def _tile_rows(tile, n, seq_len, mod):
    return min(tile, seq_len if mod.ndim == 3 else n)


def _row_spec(r, c):
    return pl.BlockSpec((r, c), lambda i: (i, 0))


def _const_spec(shape):
    nd = len(shape)
    return pl.BlockSpec(shape, lambda i: (0,) * nd)


def _mod_spec(m, r, tiles_per_seq):
    if m.ndim == 3:
        return pl.BlockSpec((None, 1, m.shape[-1]), lambda i: (i // tiles_per_seq, 0, 0))
    return _row_spec(r, m.shape[-1])


def _tab_spec(t, r, tiles_per_seq, per_token):
    if per_token:
        return _row_spec(r, t.shape[-1])
    return pl.BlockSpec((r, t.shape[-1]), lambda i: (i % tiles_per_seq, 0))


def _even_proj_kernel(x_ref, sc_ref, sh_ref, cq_ref, sq_ref, ck_ref, skn_ref, skp_ref,
                      w1_ref, qg_ref, wq_ref, kvg_ref, wcat_ref,
                      qcat_ref, kcat_ref, ckv_ref, kr_ref, sbq_ref, sbk_ref, sbv_ref,
                      sbk16_ref, sbv16_ref, *, q_lora, kv_lora, sb_w):
    h = x_ref[...] * (1.0 + sc_ref[...]) + sh_ref[...]
    y = _dot(h.astype(BF16), w1_ref[...])
    o = 0
    q_a = y[:, o:o + q_lora]; o += q_lora
    kv_a = y[:, o:o + kv_lora]; o += kv_lora
    sbq = y[:, o:o + sb_w]; o += sb_w
    sbk = y[:, o:o + sb_w]; o += sb_w
    sbv = y[:, o:o + sb_w]; o += sb_w
    kr = y[:, o:o + LANES]

    q = _dot(_rms(q_a, qg_ref[...]).astype(BF16), wq_ref[...])
    n_nope = MLA_HEADS * MLA_NOPE
    x1 = q[:, n_nope:n_nope + LANES]
    x2 = q[:, n_nope + LANES:n_nope + 2 * LANES]
    cos, sin = cq_ref[...], sq_ref[...]
    qin = jnp.concatenate([q[:, :n_nope], x1 * cos - x2 * sin, x2 * cos + x1 * sin], axis=1)
    qcat_ref[...] = _dot(qin.astype(BF16), wcat_ref[...]).astype(BF16)

    ckv = _rms(kv_a, kvg_ref[...])
    ckv_ref[...] = ckv
    kro = (kr * ck_ref[...] + pltpu.roll(kr, LANES - ROPE_HALF, 1) * skn_ref[...]
           + pltpu.roll(kr, ROPE_HALF, 1) * skp_ref[...])
    kr_ref[...] = kro[:, :MLA_ROPE]
    kcat_ref[...] = jnp.concatenate([ckv, kro], axis=1).astype(BF16)

    sbq_ref[...] = sbq.astype(BF16)
    sbk_ref[...] = sbk
    sbv_ref[...] = sbv
    sbk16_ref[...] = sbk.astype(BF16)
    sbv16_ref[...] = sbv.astype(BF16)


def _even_proj(x, sc, sh, tabs, per_token, seq_len, w1, qg, wq, kvg, wcat):
    n, d = x.shape
    r = _tile_rows(ROW_TILE, n, seq_len, sc)
    tps = max(seq_len // r, 1)
    q_lora, kv_lora = qg.shape[-1], kvg.shape[-1]
    sb_w = SB_HEADS * SB_DIM
    qw = MLA_HEADS * MLA_QPAD
    kern = functools.partial(_even_proj_kernel, q_lora=q_lora, kv_lora=kv_lora, sb_w=sb_w)
    out_shapes = [
        jax.ShapeDtypeStruct((n, qw), BF16),
        jax.ShapeDtypeStruct((n, kv_lora + LANES), BF16),
        jax.ShapeDtypeStruct((n, kv_lora), F32),
        jax.ShapeDtypeStruct((n, MLA_ROPE), F32),
        jax.ShapeDtypeStruct((n, sb_w), BF16),
        jax.ShapeDtypeStruct((n, sb_w), F32),
        jax.ShapeDtypeStruct((n, sb_w), F32),
        jax.ShapeDtypeStruct((n, sb_w), BF16),
        jax.ShapeDtypeStruct((n, sb_w), BF16),
    ]
    return pl.pallas_call(
        kern,
        grid=(n // r,),
        in_specs=[_row_spec(r, d), _mod_spec(sc, r, tps), _mod_spec(sh, r, tps)]
        + [_tab_spec(t, r, tps, per_token) for t in tabs]
        + [_const_spec(w1.shape), _const_spec(qg.shape), _const_spec(wq.shape),
           _const_spec(kvg.shape), _const_spec(wcat.shape)],
        out_specs=[_row_spec(r, s.shape[1]) for s in out_shapes],
        out_shape=out_shapes,
        compiler_params=_params("arbitrary"),
        name="even_proj",
    )(x, sc, sh, *tabs, w1, qg, wq, kvg, wcat)


def _mla_prompt_kernel(q_ref, k_ref, o_ref, m_sc, l_sc, acc_sc, *, tile, kv_lora, scale):
    qi = pl.program_id(2)
    q = q_ref[...]
    m_sc[...] = jnp.full(m_sc.shape, -jnp.inf, F32)
    l_sc[...] = jnp.zeros(l_sc.shape, F32)
    acc_sc[...] = jnp.zeros(acc_sc.shape, F32)

    def keys(kb):
        return k_ref[pl.ds(pl.multiple_of(kb * tile, tile), tile), :]

    def scores(kb):
        return _dot_nt(q, keys(kb))

    def absorb(s, kb, masked):
        s = s * scale
        if masked:
            row = lax.broadcasted_iota(jnp.int32, s.shape, 0)
            col = lax.broadcasted_iota(jnp.int32, s.shape, 1)
            s = jnp.where(col <= row, s, -jnp.inf)
        m_prev = m_sc[...]
        m_new = jnp.maximum(m_prev, jnp.max(s, axis=1, keepdims=True))
        alpha = jnp.exp(m_prev - m_new)
        p = jnp.exp(s - m_new)
        l_sc[...] = alpha * l_sc[...] + jnp.sum(p, axis=1, keepdims=True)
        acc_sc[...] = alpha * acc_sc[...] + _dot(p.astype(BF16), keys(kb)[:, :kv_lora])
        m_sc[...] = m_new

    def body(kb, s):
        s_next = scores(kb + 1)
        absorb(s, kb, False)
        return s_next

    s = lax.fori_loop(0, qi, body, scores(0))
    absorb(s, qi, True)
    o_ref[...] = (acc_sc[...] / l_sc[...]).astype(BF16)


def _mla_prompt(qcat, kcat, batch, seq):
    n = qcat.shape[0]
    kv_lora = kcat.shape[1] - LANES
    tile = min(ATT_TILE, seq)
    nq = seq // tile
    scale = (MLA_NOPE + MLA_ROPE) ** -0.5
    kern = functools.partial(_mla_prompt_kernel, tile=tile, kv_lora=kv_lora, scale=scale)
    return pl.pallas_call(
        kern,
        grid=(batch, MLA_HEADS, nq),
        in_specs=[
            pl.BlockSpec((tile, MLA_QPAD), lambda b, h, i: (b * nq + i, h)),
            pl.BlockSpec((seq, kcat.shape[1]), lambda b, h, i: (b, 0)),
        ],
        out_specs=pl.BlockSpec((tile, kv_lora), lambda b, h, i: (b * nq + i, h)),
        out_shape=jax.ShapeDtypeStruct((n, MLA_HEADS * kv_lora), BF16),
        scratch_shapes=[pltpu.VMEM((tile, 1), F32), pltpu.VMEM((tile, 1), F32),
                        pltpu.VMEM((tile, kv_lora), F32)],
        compiler_params=_params("arbitrary", "arbitrary", "arbitrary"),
        name="mla_prompt",
    )(qcat, kcat)


def _suffix_matrix(n):
    row = lax.broadcasted_iota(jnp.int32, (n, n), 0)
    col = lax.broadcasted_iota(jnp.int32, (n, n), 1)
    return jnp.where(row > col, 1.0, 0.0).astype(BF16)


def _sb_log_terms(z, valid):
    t = jnp.log(1.0 + jnp.exp(-jnp.abs(z)))
    log_beta = jnp.minimum(z, 0.0) - t
    log_stay = log_beta - z
    if valid is not None:
        log_stay = jnp.where(valid, log_stay, 0.0)
    return log_beta, log_stay


def _sb_weights(log_beta, within, carry, valid):
    w = jnp.exp(log_beta + (carry + within))
    if valid is not None:
        w = jnp.where(valid, w, 0.0)
    return w


def _sb_prompt_kernel(q_ref, k_ref, v_ref, o_ref, *, tq, tk, scale):
    qi = pl.program_id(2)
    suffix = _suffix_matrix(tk)
    q_pos = qi * tq + lax.broadcasted_iota(jnp.int32, (tq, tk), 0)
    col = lax.broadcasted_iota(jnp.int32, (tq, tk), 1)
    heads = LANES // SB_DIM
    sls = [slice(hh * SB_DIM, (hh + 1) * SB_DIM) for hh in range(heads)]
    qs = [(q_ref[:, sl].astype(F32) * scale).astype(BF16) for sl in sls]

    def rows_of(kb):
        return pl.ds(pl.multiple_of(kb * tk, tk), tk)

    def scores(kb):
        k2 = k_ref[rows_of(kb), :]
        return tuple(_dot_nt(qs[hh], k2[:, sls[hh]]) for hh in range(heads))

    def absorb(z, kb, state, masked):
        v2 = v_ref[rows_of(kb), :]
        valid = (kb * tk + col) < q_pos if masked else None
        terms = []
        for hh in range(heads):
            log_beta, log_stay = _sb_log_terms(z[hh], valid)
            terms.append((log_beta, log_stay, _split_dot(log_stay, suffix)))
        new = []
        for hh in range(heads):
            carry, acc = state[hh]
            log_beta, log_stay, within = terms[hh]
            w = _sb_weights(log_beta, within, carry, valid)
            new.append((carry + jnp.sum(log_stay, axis=1, keepdims=True),
                        acc + _dot(w.astype(BF16), v2[:, sls[hh]])))
        return tuple(new)

    top = (qi + 1) * (tq // tk) - 1
    n_masked = tq // tk
    state = tuple((jnp.zeros((tq, 1), F32), jnp.zeros((tq, SB_DIM), F32)) for _ in range(heads))
    z = scores(top)
    for i in range(n_masked):
        z_next = scores(jnp.maximum(top - 1 - i, 0))
        state = absorb(z, top - i, state, True)
        z = z_next

    def body(i, c):
        kb = top - n_masked - i
        z_next = scores(jnp.maximum(kb - 1, 0))
        return z_next, absorb(c[0], kb, c[1], False)

    z, state = lax.fori_loop(0, top + 1 - n_masked, body, (z, state))
    o_ref[...] = jnp.concatenate([acc for _, acc in state], axis=1).astype(BF16)


def _sb_prompt(sbq, sbk, sbv, batch, seq):
    n, w = sbq.shape
    tk = min(SB_TILE, seq)
    tile = min(2 * tk, seq)
    nq = seq // tile
    kern = functools.partial(_sb_prompt_kernel, tq=tile, tk=tk, scale=SB_DIM ** -0.5)
    return pl.pallas_call(
        kern,
        grid=(batch, w // LANES, nq),
        in_specs=[
            pl.BlockSpec((tile, LANES), lambda b, h, i: (b * nq + i, h)),
            pl.BlockSpec((seq, LANES), lambda b, h, i: (b, h)),
            pl.BlockSpec((seq, LANES), lambda b, h, i: (b, h)),
        ],
        out_specs=pl.BlockSpec((tile, LANES), lambda b, h, i: (b * nq + i, h)),
        out_shape=jax.ShapeDtypeStruct((n, w), BF16),
        compiler_params=_params("arbitrary", "arbitrary", "arbitrary"),
        name="sb_prompt",
    )(sbq, sbk, sbv)


def _pad_rows(x, rows):
    return jnp.concatenate([x, jnp.zeros((rows - x.shape[0], x.shape[1]), x.dtype)], axis=0)


def _sample_attention_kernel(pt_ref, q_ref, kn_ref, sq_ref, skn_ref, svn_ref, *refs,
                             n_pages, page, kv_lora, mla_scale, sb_scale, dec):
    del pt_ref
    ckv_refs = refs[:n_pages]
    krt_refs = refs[n_pages:2 * n_pages]
    sk_refs = refs[2 * n_pages:3 * n_pages]
    sv_refs = refs[3 * n_pages:4 * n_pages]
    o_ref, so_ref = refs[4 * n_pages:4 * n_pages + 2]
    (q_sc, kv_sc, krt_sc, m_sc, l_sc, acc_sc,
     sq_sc, sk_sc, sv_sc, carry_sc, sacc_sc) = refs[4 * n_pages + 2:]
    g = pl.program_id(1)
    rows = MLA_HEADS * dec
    srows = SB_HEADS * dec
    width = SB_HEADS * SB_DIM
    suffix = _suffix_matrix(page)

    def softmax_update(parts):
        m_prev = m_sc[...]
        m_new = m_prev
        for s, _ in parts:
            m_new = jnp.maximum(m_new, jnp.max(s, axis=1, keepdims=True))
        alpha = jnp.exp(m_prev - m_new)
        l_new = alpha * l_sc[...]
        acc = alpha * acc_sc[...]
        for s, values in parts:
            p = jnp.exp(s - m_new)
            l_new = l_new + jnp.sum(p, axis=1, keepdims=True)
            acc = acc + _dot(p.astype(BF16), values)
        l_sc[...] = l_new
        acc_sc[...] = acc
        m_sc[...] = m_new

    def head_mask(shape):
        r = lax.broadcasted_iota(jnp.int32, shape, 0)
        c = lax.broadcasted_iota(jnp.int32, shape, 1)
        return (r // dec) == (c // SB_DIM)

    def log_terms(z, valid, n):
        zr = jnp.concatenate([z[:, c * page:(c + 1) * page] for c in range(n)], axis=0)
        log_beta, log_stay = _sb_log_terms(zr, valid)
        return log_beta, log_stay, _split_dot(log_stay, suffix)

    def sb_weights(terms, carry, valid, n):
        log_beta, log_stay, within = terms
        total = jnp.sum(log_stay, axis=1, keepdims=True)
        carries = []
        for c in range(n):
            carries.append(carry)
            carry = carry + total[c * srows:(c + 1) * srows]
        w = _sb_weights(log_beta, within, jnp.concatenate(carries, axis=0), valid)
        w = jnp.concatenate([w[c * srows:(c + 1) * srows] for c in range(n)], axis=1)
        return w.astype(BF16), carry

    @pl.when(g == 0)
    def _():
        t = lax.broadcasted_iota(jnp.int32, (rows, page), 0) % dec
        j = lax.broadcasted_iota(jnp.int32, (rows, page), 1)
        qf = q_ref[0].astype(F32)
        q = jnp.concatenate([qf[:, h * MLA_QPAD:(h + 1) * MLA_QPAD] for h in range(MLA_HEADS)], axis=0)
        q = q.astype(BF16)
        q_sc[...] = q
        m_sc[...] = jnp.full(m_sc.shape, -jnp.inf, F32)
        l_sc[...] = jnp.zeros(l_sc.shape, F32)
        acc_sc[...] = jnp.zeros(acc_sc.shape, F32)
        kn = _pad_rows(kn_ref[0].astype(F32), page).astype(BF16)
        s = _dot_nt(q, kn) * mla_scale
        softmax_update([(jnp.where(j <= t, s, -jnp.inf), kn[:, :kv_lora])])
        sqf = sq_ref[0].astype(F32)
        sq = jnp.concatenate([sqf] * SB_HEADS, axis=0)
        sq = jnp.where(head_mask((srows, width)), sq, 0.0).astype(BF16)
        sq_sc[...] = sq
        skn = _pad_rows(skn_ref[0].astype(F32), page).astype(BF16)
        svn = _pad_rows(svn_ref[0].astype(F32), page).astype(BF16)
        valid = j < t
        w, carry = sb_weights(log_terms(_dot_nt(sq, skn) * sb_scale, valid, 1),
                              jnp.zeros((srows, 1), F32), valid, 1)
        carry_sc[...] = carry
        sacc_sc[...] = _dot(w, svn)

    q = q_sc[...]
    sq = sq_sc[...]
    half = n_pages // 2
    groups = [slice(0, half * page), slice(half * page, n_pages * page)] if half else [slice(0, n_pages * page)]
    parts, terms = [], []
    for keys in groups:
        n = (keys.stop - keys.start) // page
        for p in range(keys.start // page, keys.stop // page):
            kv_sc[p * page:(p + 1) * page, :] = ckv_refs[p][...].astype(BF16)
            krt_sc[:, p * page:(p + 1) * page] = krt_refs[p][...].astype(BF16)
        kv = kv_sc[keys, :]
        s = _dot_nt(q[:, :kv_lora], kv) + _dot(q[:, kv_lora:kv_lora + MLA_ROPE], krt_sc[:, keys])
        parts.append((s * mla_scale, kv))
        for p in range(keys.start // page, keys.stop // page):
            sk_sc[:, p * page:(p + 1) * page] = sk_refs[p][...].astype(BF16)
            sv_sc[:, p * page:(p + 1) * page] = sv_refs[p][...].astype(BF16)
        terms.append(log_terms(_dot(sq, sk_sc[:, keys]) * sb_scale, None, n))
    softmax_update(parts)
    carry = carry_sc[...]
    sacc = sacc_sc[...]
    for gi, keys in enumerate(groups):
        w, carry = sb_weights(terms[gi], carry, None, (keys.stop - keys.start) // page)
        sacc = sacc + _dot_nt(w, sv_sc[:, keys])
    carry_sc[...] = carry
    sacc_sc[...] = sacc

    @pl.when(g == pl.num_programs(1) - 1)
    def _():
        o = acc_sc[...] / l_sc[...]
        o_ref[0] = jnp.concatenate([o[h * dec:(h + 1) * dec] for h in range(MLA_HEADS)], axis=1).astype(BF16)
        sacc = jnp.where(head_mask((srows, width)), sacc_sc[...], 0.0)
        so = sacc[0:dec]
        for h in range(1, SB_HEADS):
            so = so + sacc[h * dec:(h + 1) * dec]
        so_ref[0] = so.astype(BF16)


def _page_specs(layer, n_pages, pages_total, block):
    specs = []
    for p in range(n_pages):
        def index(b, g, pt, p=p):
            return (layer, pt[b * pages_total + pages_total - 1 - (g * n_pages + p)], 0, 0)
        specs.append(pl.BlockSpec((None, None) + block, index))
    return specs


def _sample_attention(qcat, kcat_new, sbq, sbk_new, sbv_new, layer, pool_ckv, pool_krt, pool_kt, pool_vt,
                      page_table, dec):
    n, qw = qcat.shape
    width = sbq.shape[1]
    dec_batch, pages_total = page_table.shape
    page, kv_lora = pool_ckv.shape[2:]
    n_pages = min(PAGES_PER_STEP, pages_total)
    rows = MLA_HEADS * dec
    srows = SB_HEADS * dec
    kern = functools.partial(_sample_attention_kernel, n_pages=n_pages, page=page, kv_lora=kv_lora,
                             mla_scale=(MLA_NOPE + MLA_ROPE) ** -0.5, sb_scale=SB_DIM ** -0.5, dec=dec)

    def seq_spec(cols):
        return pl.BlockSpec((1, dec, cols), lambda b, g, pt: (b, 0, 0))

    grid_spec = pltpu.PrefetchScalarGridSpec(
        num_scalar_prefetch=1,
        grid=(dec_batch, pages_total // n_pages),
        in_specs=[seq_spec(qw), seq_spec(kcat_new.shape[1]), seq_spec(width), seq_spec(width), seq_spec(width)]
        + _page_specs(layer, n_pages, pages_total, (page, kv_lora))
        + _page_specs(layer, n_pages, pages_total, (MLA_ROPE, page))
        + _page_specs(layer, n_pages, pages_total, (width, page))
        + _page_specs(layer, n_pages, pages_total, (width, page)),
        out_specs=[seq_spec(MLA_HEADS * kv_lora), seq_spec(width)],
        scratch_shapes=[pltpu.VMEM((rows, MLA_QPAD), BF16),
                        pltpu.VMEM((n_pages * page, kv_lora), BF16),
                        pltpu.VMEM((MLA_ROPE, n_pages * page), BF16),
                        pltpu.VMEM((rows, 1), F32), pltpu.VMEM((rows, 1), F32),
                        pltpu.VMEM((rows, kv_lora), F32),
                        pltpu.VMEM((srows, width), BF16),
                        pltpu.VMEM((width, n_pages * page), BF16),
                        pltpu.VMEM((width, n_pages * page), BF16),
                        pltpu.VMEM((srows, 1), F32),
                        pltpu.VMEM((srows, width), F32)],
    )

    def per_seq(a):
        return a.reshape(dec_batch, dec, a.shape[1])

    olat, osb = pl.pallas_call(
        kern,
        grid_spec=grid_spec,
        out_shape=[jax.ShapeDtypeStruct((dec_batch, dec, MLA_HEADS * kv_lora), BF16),
                   jax.ShapeDtypeStruct((dec_batch, dec, width), BF16)],
        compiler_params=_params("arbitrary", "arbitrary"),
        name="sample_attention",
    )(page_table.reshape(-1), per_seq(qcat), per_seq(kcat_new), per_seq(sbq), per_seq(sbk_new),
      per_seq(sbv_new), *([pool_ckv] * n_pages), *([pool_krt] * n_pages), *([pool_kt] * n_pages),
      *([pool_vt] * n_pages))
    return olat.reshape(n, MLA_HEADS * kv_lora), osb.reshape(n, width)


def _router_gates(h2, rw_ref, rb_ref):
    logits = lax.dot_general(rw_ref[...], h2, (((1,), (1,)), ((), ())),
                             precision=lax.Precision.HIGHEST, preferred_element_type=F32)
    aff = _sigmoid(logits)
    sel = aff + rb_ref[...]
    per = PER_GROUP
    s = [sel[j * N_GROUPS:(j + 1) * N_GROUPS] for j in range(per)]
    a = [aff[j * N_GROUPS:(j + 1) * N_GROUPS] for j in range(per)]
    score = None
    for i in range(per):
        for j in range(i + 1, per):
            pair = s[i] + s[j]
            score = pair if score is None else jnp.maximum(score, pair)
    best = jnp.max(score, axis=0, keepdims=True)
    gidx = lax.broadcasted_iota(jnp.int32, score.shape, 0)
    first = jnp.min(jnp.where(score == best, gidx, N_GROUPS), axis=0, keepdims=True)
    in_group = gidx == first
    picked = []
    for j in range(per):
        rank = jnp.zeros(score.shape, F32)
        for k in range(per):
            if k == j:
                continue
            ahead = (s[k] >= s[j]) if k < j else (s[k] > s[j])
            rank = rank + jnp.where(ahead, 1.0, 0.0)
        picked.append(jnp.where(in_group, jnp.where(rank < 2.0, a[j], 0.0), 0.0))
    total = None
    for j in range(per):
        part = jnp.sum(picked[j], axis=0, keepdims=True)
        total = part if total is None else total + part
    gates_t = jnp.concatenate([p / total for p in picked], axis=0)
    gates_t = _pad_rows(gates_t, LANES)
    return gates_t.T


def _mix_tail(x, mix, g1, sc2, sh2, lng_ref, lnb_ref, rw_ref, rb_ref, alpha, x1_ref, h2_ref, gates_ref):
    x1 = _layer_norm(alpha * x + (1.0 + g1) * mix, lng_ref[...], lnb_ref[...])
    h2 = x1 * (1.0 + sc2) + sh2
    x1_ref[...] = x1
    h2_ref[...] = h2.astype(BF16)
    gates_ref[...] = _router_gates(h2, rw_ref, rb_ref)


def _even_post_kernel(olat_ref, osb_ref, x_ref, g1_ref, sc2_ref, sh2_ref, bdv_ref, wo_ref,
                      lng_ref, lnb_ref, rw_ref, rb_ref, x1_ref, h2_ref, gates_ref, *, alpha):
    o_mla = _dot(olat_ref[...], bdv_ref[...]).astype(BF16)
    nv = o_mla.shape[1]
    mix = _dot(o_mla, wo_ref[:nv, :]) + _dot(osb_ref[...], wo_ref[nv:, :])
    _mix_tail(x_ref[...], mix, g1_ref[...], sc2_ref[...], sh2_ref[...], lng_ref, lnb_ref,
              rw_ref, rb_ref, alpha, x1_ref, h2_ref, gates_ref)


def _odd_post_kernel(o_ref, r_ref, x_ref, g1_ref, sc2_ref, sh2_ref, ng_ref, wo_ref,
                     lng_ref, lnb_ref, rw_ref, rb_ref, x1_ref, h2_ref, gates_ref, *, alpha, dv):
    o = o_ref[...]
    parts = []
    for h in range(GLA_HEADS):
        sl = slice(h * dv, (h + 1) * dv)
        parts.append(_rms(o[:, sl], ng_ref[:, sl]))
    gated = jnp.concatenate(parts, axis=1) * _silu(r_ref[...])
    mix = _dot(gated.astype(BF16), wo_ref[...])
    _mix_tail(x_ref[...], mix, g1_ref[...], sc2_ref[...], sh2_ref[...], lng_ref, lnb_ref,
              rw_ref, rb_ref, alpha, x1_ref, h2_ref, gates_ref)


def _post_call(kern, name, acts, x, mods, consts, seq_len):
    n, d = x.shape
    r = _tile_rows(ROW_TILE, n, seq_len, mods[0])
    tps = max(seq_len // r, 1)
    out_shapes = [jax.ShapeDtypeStruct((n, d), F32), jax.ShapeDtypeStruct((n, d), BF16),
                  jax.ShapeDtypeStruct((n, LANES), F32)]
    return pl.pallas_call(
        kern,
        grid=(n // r,),
        in_specs=[_row_spec(r, a.shape[1]) for a in acts] + [_row_spec(r, d)]
        + [_mod_spec(m, r, tps) for m in mods] + [_const_spec(c.shape) for c in consts],
        out_specs=[_row_spec(r, s.shape[1]) for s in out_shapes],
        out_shape=out_shapes,
        compiler_params=_params("arbitrary"),
        name=name,
    )(*acts, x, *mods, *consts)


def _moe_kernel(h_ref, gates_ref, x1_ref, g2_ref, wg_ref, wu_ref, wd_ref, lng_ref, lnb_ref,
                o_ref, acc_sc, *, alpha):
    e = pl.program_id(1)

    @pl.when(e == 0)
    def _():
        acc_sc[...] = jnp.zeros(acc_sc.shape, F32)

    h = h_ref[...]
    a = _silu(_dot(h, wg_ref[0])) * _dot(h, wu_ref[0])
    y = _dot(a.astype(BF16), wd_ref[0])
    col = (e % PER_GROUP) * N_GROUPS + e // PER_GROUP
    gates = gates_ref[...]
    lane = lax.broadcasted_iota(jnp.int32, gates.shape, 1)
    gate = jnp.sum(jnp.where(lane == col, gates, 0.0), axis=1, keepdims=True)
    acc_sc[...] += gate * y

    @pl.when(e == pl.num_programs(1) - 1)
    def _():
        o_ref[...] = _layer_norm(alpha * x1_ref[...] + (1.0 + g2_ref[...]) * acc_sc[...],
                                 lng_ref[...], lnb_ref[...])


def _moe(h2, gates, x1, g2, wg, wu, wd, lng, lnb, seq_len, alpha):
    n, d = x1.shape
    r = _tile_rows(MOE_ROW_TILE, n, seq_len, g2)
    tps = max(seq_len // r, 1)
    n_exp, _, de = wg.shape
    if g2.ndim == 3:
        g2_spec = pl.BlockSpec((None, 1, d), lambda i, e: (i // tps, 0, 0))
    else:
        g2_spec = pl.BlockSpec((r, d), lambda i, e: (i, 0))
    return pl.pallas_call(
        functools.partial(_moe_kernel, alpha=alpha),
        grid=(n // r, n_exp),
        in_specs=[
            pl.BlockSpec((r, d), lambda i, e: (i, 0)),
            pl.BlockSpec((r, LANES), lambda i, e: (i, 0)),
            pl.BlockSpec((r, d), lambda i, e: (i, 0)),
            g2_spec,
            pl.BlockSpec((1, d, de), lambda i, e: (e, 0, 0)),
            pl.BlockSpec((1, d, de), lambda i, e: (e, 0, 0)),
            pl.BlockSpec((1, de, d), lambda i, e: (e, 0, 0)),
            pl.BlockSpec((1, d), lambda i, e: (0, 0)),
            pl.BlockSpec((1, d), lambda i, e: (0, 0)),
        ],
        out_specs=pl.BlockSpec((r, d), lambda i, e: (i, 0)),
        out_shape=jax.ShapeDtypeStruct((n, d), F32),
        scratch_shapes=[pltpu.VMEM((r, d), F32)],
        compiler_params=_params("arbitrary", "arbitrary"),
        name="moe",
    )(h2, gates, x1, g2, wg, wu, wd, lng, lnb)


def _odd_proj_kernel(x_ref, sc_ref, sh_ref, w_ref, wkt_ref, wgu_ref, wgut_ref, bg_ref, bgt_ref,
                     qd_ref, kdt_ref, ket_ref, dec_ref, v_ref, r_ref, *, hk, hv, chunk, qscale):
    h = (x_ref[...] * (1.0 + sc_ref[...]) + sh_ref[...]).astype(BF16)
    rows = h.shape[0]
    y = _dot(h, w_ref[...])
    q = y[:, :hk] * qscale
    v = y[:, hk:hk + hv]
    r = y[:, hk + hv:hk + 2 * hv]
    glr = y[:, hk + 2 * hv:hk + 2 * hv + LANES].astype(BF16)
    kt = _dot_nt(wkt_ref[...], h)

    def log_sigmoid(z):
        return jnp.minimum(z, 0.0) - jnp.log1p(jnp.exp(-jnp.abs(z)))

    log_a = log_sigmoid(_dot(glr, wgu_ref[...]) + bg_ref[...]) * (1.0 / GLA_TAU)
    log_at = log_sigmoid(_dot_nt(wgut_ref[...], glr) + bgt_ref[...]) * (1.0 / GLA_TAU)

    i = lax.broadcasted_iota(jnp.int32, (rows, rows), 0)
    j = lax.broadcasted_iota(jnp.int32, (rows, rows), 1)
    same = (i // chunk) == (j // chunk)
    lower = jnp.where(same & (j <= i), 1.0, 0.0).astype(BF16)
    upper = jnp.where(same & (i <= j), 1.0, 0.0).astype(BF16)
    block = jnp.where(same, 1.0, 0.0).astype(BF16)

    hi = log_a.astype(BF16)
    lo = (log_a - hi.astype(F32)).astype(BF16)
    g_cum = _dot(lower, hi) + _dot(lower, lo)
    g_cum_t = _split_dot(log_at, upper)
    g_tot_t = _split_dot(log_at, block)

    qd_ref[...] = (q * jnp.exp(g_cum)).astype(BF16)
    kdt_ref[...] = (kt * jnp.exp(-g_cum_t)).astype(BF16)
    ket_ref[...] = (kt * jnp.exp(g_tot_t - g_cum_t)).astype(BF16)
    dec_ref[...] = jnp.exp(g_tot_t)
    v_ref[...] = v.astype(BF16)
    r_ref[...] = r


def _odd_proj(x, sc, sh, seq_len, w, wkt, wgu, wgut, bg, bgt, chunk):
    n, d = x.shape
    r = _tile_rows(ROW_TILE, n, seq_len, sc)
    tps = max(seq_len // r, 1)
    hk = wkt.shape[0]
    hv = (w.shape[1] - hk - LANES) // 2
    kern = functools.partial(_odd_proj_kernel, hk=hk, hv=hv, chunk=chunk, qscale=(hk // GLA_HEADS) ** -0.5)
    col_spec = pl.BlockSpec((hk, r), lambda i: (0, i))
    out_shapes = [
        jax.ShapeDtypeStruct((n, hk), BF16),
        jax.ShapeDtypeStruct((hk, n), BF16),
        jax.ShapeDtypeStruct((hk, n), BF16),
        jax.ShapeDtypeStruct((hk, n), F32),
        jax.ShapeDtypeStruct((n, hv), BF16),
        jax.ShapeDtypeStruct((n, hv), F32),
    ]
    return pl.pallas_call(
        kern,
        grid=(n // r,),
        in_specs=[_row_spec(r, d), _mod_spec(sc, r, tps), _mod_spec(sh, r, tps)]
        + [_const_spec(c.shape) for c in (w, wkt, wgu, wgut, bg, bgt)],
        out_specs=[_row_spec(r, hk), col_spec, col_spec, col_spec, _row_spec(r, hv), _row_spec(r, hv)],
        out_shape=out_shapes,
        compiler_params=_params("arbitrary"),
        name="odd_proj",
    )(x, sc, sh, w, wkt, wgu, wgut, bg, bgt)


def _gla_chunk(qd, kdt, ket, dec, v, state, tril):
    att = jnp.where(tril, _dot(qd, kdt), 0.0)
    o = _dot(qd, state.astype(BF16)) + _dot(att.astype(BF16), v)
    return o, state * dec + _dot(ket, v)


def _gla_prompt_kernel(qd_ref, kdt_ref, ket_ref, dec_ref, v_ref, o_ref, s_ref, state_sc, *, chunk):
    g = pl.program_id(2)

    @pl.when(g == 0)
    def _():
        state_sc[...] = jnp.zeros(state_sc.shape, F32)

    i = lax.broadcasted_iota(jnp.int32, (chunk, chunk), 0)
    j = lax.broadcasted_iota(jnp.int32, (chunk, chunk), 1)
    tril = j <= i
    state = state_sc[...]
    group = qd_ref.shape[0]
    for c in range(group // chunk):
        sl = slice(c * chunk, (c + 1) * chunk)
        o, state = _gla_chunk(qd_ref[sl, :], kdt_ref[:, sl], ket_ref[:, sl],
                              dec_ref[:, c * chunk:c * chunk + 1], v_ref[sl, :], state, tril)
        o_ref[sl, :] = o
    state_sc[...] = state

    @pl.when(g == pl.num_programs(2) - 1)
    def _():
        s_ref[0, 0] = state


def _gla_prompt(qd, kdt, ket, dec, v, batch, seq, chunk):
    n, hk = qd.shape
    hv = v.shape[1]
    dk, dv = hk // GLA_HEADS, hv // GLA_HEADS
    group = min(LANES, seq)
    ng = seq // group
    row_k = pl.BlockSpec((group, dk), lambda b, h, g: (b * ng + g, h))
    row_v = pl.BlockSpec((group, dv), lambda b, h, g: (b * ng + g, h))
    col_k = pl.BlockSpec((dk, group), lambda b, h, g: (h, b * ng + g))
    return pl.pallas_call(
        functools.partial(_gla_prompt_kernel, chunk=chunk),
        grid=(batch, GLA_HEADS, ng),
        in_specs=[row_k, col_k, col_k, col_k, row_v],
        out_specs=[row_v, pl.BlockSpec((1, 1, dk, dv), lambda b, h, g: (b, h, 0, 0))],
        out_shape=[jax.ShapeDtypeStruct((n, hv), F32),
                   jax.ShapeDtypeStruct((batch, GLA_HEADS, dk, dv), F32)],
        scratch_shapes=[pltpu.VMEM((dk, dv), F32)],
        compiler_params=_params("arbitrary", "arbitrary", "arbitrary"),
        name="gla_prompt",
    )(qd, kdt, ket, dec, v)


def _gla_sample_kernel(qd_ref, kdt_ref, ket_ref, dec_ref, v_ref, s0_ref, o_ref, s_ref, *, chunk):
    i = lax.broadcasted_iota(jnp.int32, (chunk, chunk), 0)
    j = lax.broadcasted_iota(jnp.int32, (chunk, chunk), 1)
    tril = j <= i
    group = qd_ref.shape[0]
    for c in range(group // chunk):
        sl = slice(c * chunk, (c + 1) * chunk)
        o, state = _gla_chunk(qd_ref[sl, :].astype(BF16), kdt_ref[:, sl].astype(BF16),
                              ket_ref[:, sl].astype(BF16), dec_ref[:, c * chunk:c * chunk + 1],
                              v_ref[sl, :].astype(BF16), s0_ref[c, 0], tril)
        o_ref[sl, :] = o
        s_ref[c, 0] = state


def _gla_sample(qd, kdt, ket, dec, v, s0, chunk):
    n, hk = qd.shape
    hv = v.shape[1]
    dk, dv = hk // GLA_HEADS, hv // GLA_HEADS
    group = min(LANES, n)
    seqs = group // chunk
    row_k = pl.BlockSpec((group, dk), lambda g, h: (g, h))
    row_v = pl.BlockSpec((group, dv), lambda g, h: (g, h))
    col_k = pl.BlockSpec((dk, group), lambda g, h: (h, g))
    st = pl.BlockSpec((seqs, 1, dk, dv), lambda g, h: (g, h, 0, 0))
    return pl.pallas_call(
        functools.partial(_gla_sample_kernel, chunk=chunk),
        grid=(n // group, GLA_HEADS),
        in_specs=[row_k, col_k, col_k, col_k, row_v, st],
        out_specs=[row_v, st],
        out_shape=[jax.ShapeDtypeStruct((n, hv), F32),
                   jax.ShapeDtypeStruct(s0.shape, F32)],
        compiler_params=_params("arbitrary", "arbitrary"),
        name="gla_sample",
    )(qd.astype(F32), kdt.astype(F32), ket.astype(F32), dec, v.astype(F32), s0)


def _even_weights(w_in, w_q_up, w_kv_up, q_lora, kv_lora):
    d = w_in.shape[0]
    sb_w = SB_HEADS * SB_DIM
    o = 0
    q_a = w_in[:, o:o + q_lora]; o += q_lora
    kv_a = w_in[:, o:o + kv_lora]; o += kv_lora
    k_r = w_in[:, o:o + MLA_ROPE]; o += MLA_ROPE
    sb = w_in[:, o:o + 3 * sb_w]
    w1 = jnp.concatenate([q_a, kv_a, sb, k_r, jnp.zeros((d, LANES - MLA_ROPE), w_in.dtype)], axis=1)

    qh = w_q_up.reshape(q_lora, MLA_HEADS, MLA_NOPE + MLA_ROPE)
    wq = jnp.concatenate([
        qh[:, :, :MLA_NOPE].reshape(q_lora, -1),
        qh[:, :, MLA_NOPE:MLA_NOPE + ROPE_HALF].reshape(q_lora, -1),
        qh[:, :, MLA_NOPE + ROPE_HALF:].reshape(q_lora, -1)], axis=1)

    eye_h = jnp.eye(MLA_HEADS, dtype=w_kv_up.dtype)
    eye_r = jnp.eye(ROPE_HALF, dtype=w_kv_up.dtype)
    k_up = jnp.transpose(w_kv_up[:, :, :MLA_NOPE], (1, 2, 0))
    nope_rows = (k_up[:, :, None, :] * eye_h[:, None, :, None]).reshape(MLA_HEADS * MLA_NOPE, MLA_HEADS, kv_lora)
    rope_rows = (eye_h[:, None, :, None] * eye_r[None, :, None, :]).reshape(MLA_HEADS * ROPE_HALF, MLA_HEADS, ROPE_HALF)
    pad = MLA_QPAD - kv_lora - MLA_ROPE

    def z(rows, cols):
        return jnp.zeros((rows, MLA_HEADS, cols), w_kv_up.dtype)

    n_r = MLA_HEADS * ROPE_HALF
    wcat = jnp.concatenate([
        jnp.concatenate([nope_rows, z(nope_rows.shape[0], MLA_QPAD - kv_lora)], axis=2),
        jnp.concatenate([z(n_r, kv_lora), rope_rows, z(n_r, ROPE_HALF + pad)], axis=2),
        jnp.concatenate([z(n_r, kv_lora + ROPE_HALF), rope_rows, z(n_r, pad)], axis=2)], axis=0)
    wcat = wcat.reshape(-1, MLA_HEADS * MLA_QPAD)

    v_up = jnp.transpose(w_kv_up[:, :, MLA_NOPE:], (1, 0, 2))
    bdv = (v_up[:, :, None, :] * eye_h[:, None, :, None]).reshape(MLA_HEADS * kv_lora, MLA_HEADS * MLA_V)
    return w1.astype(BF16), wq.astype(BF16), wcat.astype(BF16), bdv.astype(BF16)


def _tokens_minor(pool):
    feat = 1
    for s in pool.shape[3:]:
        feat *= s
    return jnp.moveaxis(pool, 2, -1).reshape(pool.shape[:2] + (feat, pool.shape[2]))


def _rope_tables(pos):
    inv_freq = ROPE_THETA ** (-jnp.arange(ROPE_HALF, dtype=F32) / ROPE_HALF)
    ang = pos.astype(F32)[:, None] * inv_freq[None, :]
    cos, sin = jnp.cos(ang), jnp.sin(ang)
    t = pos.shape[0]
    z16 = jnp.zeros((t, ROPE_HALF), F32)
    zpad = jnp.zeros((t, LANES - MLA_ROPE), F32)
    return (jnp.tile(cos, (1, MLA_HEADS)), jnp.tile(sin, (1, MLA_HEADS)),
            jnp.concatenate([cos, cos, zpad], axis=1),
            jnp.concatenate([-sin, z16, zpad], axis=1),
            jnp.concatenate([z16, sin, zpad], axis=1))


def _odd_weights(w_in, w_gate_up, b_gate):
    d = w_in.shape[0]
    hk = w_gate_up.shape[1]
    rank = w_gate_up.shape[0]
    hv = (w_in.shape[1] - 2 * hk - rank) // 2
    q = w_in[:, :hk]
    k = w_in[:, hk:2 * hk]
    rest = w_in[:, 2 * hk:]
    w = jnp.concatenate([q, rest, jnp.zeros((d, LANES - rank), w_in.dtype)], axis=1)
    wgu = jnp.concatenate([w_gate_up, jnp.zeros((LANES - rank, hk), w_gate_up.dtype)], axis=0)
    del hv
    return (w.astype(BF16), k.T.astype(BF16), wgu.astype(BF16), wgu.T.astype(BF16),
            b_gate.reshape(1, hk), b_gate.reshape(hk, 1))


def kernel(x_prompt, x_sample, c_prompt, c_sample, cache_mla_ckv, cache_mla_krope, cache_sb_k, cache_sb_v,
           state_gla, page_table, w_in_even, q_norm_g, w_q_up, kv_norm_g, w_kv_up, w_out_even,
           w_in_odd, w_gate_up, b_gate, gla_norm_g, w_out_odd, ada_w, ada_b, ln_g, ln_b,
           router_w, router_b, w_exp_gate, w_exp_up, w_exp_down):
    batch, seq, d = x_prompt.shape
    dec_batch, dec, _ = x_sample.shape
    depth = ada_w.shape[0]
    page = cache_mla_ckv.shape[2]
    past_len = page_table.shape[1] * page
    alpha = (2.0 * depth) ** 0.25
    n_p, n_s = batch * seq, dec_batch * dec
    q_lora, kv_lora = q_norm_g.shape[1], kv_norm_g.shape[1]
    n_exp = router_w.shape[1]
    per_group = n_exp // N_GROUPS

    c_rows = batch + dec_batch
    c_pad = -c_rows % 8
    c_all = jnp.concatenate([c_prompt, c_sample, jnp.zeros((c_pad, d), F32)], axis=0)
    mod = _ada(c_all, ada_w, ada_b)

    def mods(l):
        mp = [m.reshape(batch, 1, d) for m in jnp.split(mod[l, :batch], 6, axis=-1)]
        ms = jnp.split(jnp.repeat(mod[l, batch:c_rows], dec, axis=0), 6, axis=-1)
        return mp, ms

    perm = jnp.arange(n_exp).reshape(N_GROUPS, per_group).T.reshape(-1)
    rw = router_w.T[perm]
    rb = router_b[perm].reshape(n_exp, 1)

    pos_p = jnp.arange(seq, dtype=jnp.int32)
    pos_s = past_len + jnp.arange(dec, dtype=jnp.int32)
    tabs_p = _rope_tables(pos_p)
    tabs_s = tuple(jnp.tile(t, (dec_batch, 1)) for t in _rope_tables(pos_s))

    xp = x_prompt.reshape(n_p, d)
    xs = x_sample.reshape(n_s, d)
    outs = {k: [] for k in ("ckv_p", "ckv_s", "kr_p", "kr_s", "sbk_p", "sbk_s", "sbv_p", "sbv_s", "gla_p", "gla_s")}

    for l in range(depth):
        (sh1p, sc1p, g1p, sh2p, sc2p, g2p), (sh1s, sc1s, g1s, sh2s, sc2s, g2s) = mods(l)
        lng1, lnb1 = ln_g[l, 0].reshape(1, d), ln_b[l, 0].reshape(1, d)
        lng2, lnb2 = ln_g[l, 1].reshape(1, d), ln_b[l, 1].reshape(1, d)
        if l % 2 == 0:
            e = l // 2
            w1, wq, wcat, bdv = _even_weights(w_in_even[e], w_q_up[e], w_kv_up[e], q_lora, kv_lora)
            qg, kvg = q_norm_g[e].reshape(1, -1), kv_norm_g[e].reshape(1, -1)
            wo = w_out_even[e].astype(BF16)
            (qcat_p, kcat_p, ckv_p, kr_p, sbq_p, sbk_p, sbv_p, sbk16_p, sbv16_p) = _even_proj(
                xp, sc1p, sh1p, tabs_p, False, seq, w1, qg, wq, kvg, wcat)
            (qcat_s, kcat_s, ckv_s, kr_s, sbq_s, sbk_s, sbv_s, sbk16_s, sbv16_s) = _even_proj(
                xs, sc1s, sh1s, tabs_s, True, dec, w1, qg, wq, kvg, wcat)
            olat_p = _mla_prompt(qcat_p, kcat_p, batch, seq)
            osb_p = _sb_prompt(sbq_p, sbk16_p, sbv16_p, batch, seq)
            olat_s, osb_s = _sample_attention(
                qcat_s, kcat_s, sbq_s, sbk16_s, sbv16_s, e, cache_mla_ckv, _tokens_minor(cache_mla_krope),
                _tokens_minor(cache_sb_k), _tokens_minor(cache_sb_v), page_table, dec)
            consts = (bdv, wo, lng1, lnb1, rw, rb)
            kern = functools.partial(_even_post_kernel, alpha=alpha)
            x1p, h2p, gates_p = _post_call(kern, "even_post", (olat_p, osb_p), xp, (g1p, sc2p, sh2p), consts, seq)
            x1s, h2s, gates_s = _post_call(kern, "even_post", (olat_s, osb_s), xs, (g1s, sc2s, sh2s), consts, dec)
            outs["ckv_p"].append(ckv_p.reshape(batch, seq, kv_lora))
            outs["ckv_s"].append(ckv_s.reshape(dec_batch, dec, kv_lora))
            outs["kr_p"].append(kr_p.reshape(batch, seq, MLA_ROPE))
            outs["kr_s"].append(kr_s.reshape(dec_batch, dec, MLA_ROPE))
            outs["sbk_p"].append(sbk_p.reshape(batch, seq, SB_HEADS, SB_DIM))
            outs["sbk_s"].append(sbk_s.reshape(dec_batch, dec, SB_HEADS, SB_DIM))
            outs["sbv_p"].append(sbv_p.reshape(batch, seq, SB_HEADS, SB_DIM))
            outs["sbv_s"].append(sbv_s.reshape(dec_batch, dec, SB_HEADS, SB_DIM))
        else:
            o = l // 2
            w, wkt, wgu, wgut, bg, bgt = _odd_weights(w_in_odd[o], w_gate_up[o], b_gate[o])
            hv = gla_norm_g.shape[1] * gla_norm_g.shape[2]
            ng = gla_norm_g[o].reshape(1, hv)
            wo = w_out_odd[o].astype(BF16)
            chunk_p, chunk_s = min(GLA_CHUNK, seq), min(GLA_CHUNK, dec)
            qd, kdt, ket, dcy, v, r_p = _odd_proj(xp, sc1p, sh1p, seq, w, wkt, wgu, wgut, bg, bgt, chunk_p)
            o_p, st_p = _gla_prompt(qd, kdt, ket, dcy, v, batch, seq, chunk_p)
            qd, kdt, ket, dcy, v, r_s = _odd_proj(xs, sc1s, sh1s, dec, w, wkt, wgu, wgut, bg, bgt, chunk_s)
            o_s, st_s = _gla_sample(qd, kdt, ket, dcy, v, state_gla[o], chunk_s)
            consts = (ng, wo, lng1, lnb1, rw, rb)
            kern = functools.partial(_odd_post_kernel, alpha=alpha, dv=gla_norm_g.shape[2])
            x1p, h2p, gates_p = _post_call(kern, "odd_post", (o_p, r_p), xp, (g1p, sc2p, sh2p), consts, seq)
            x1s, h2s, gates_s = _post_call(kern, "odd_post", (o_s, r_s), xs, (g1s, sc2s, sh2s), consts, dec)
            outs["gla_p"].append(st_p)
            outs["gla_s"].append(st_s)
        wg, wu, wd = w_exp_gate[l].astype(BF16), w_exp_up[l].astype(BF16), w_exp_down[l].astype(BF16)
        xp = _moe(h2p, gates_p, x1p, g2p, wg, wu, wd, lng2, lnb2, seq, alpha)
        xs = _moe(h2s, gates_s, x1s, g2s, wg, wu, wd, lng2, lnb2, dec, alpha)

    return (xp.reshape(batch, seq, d), xs.reshape(dec_batch, dec, d),
            jnp.stack(outs["ckv_p"]), jnp.stack(outs["ckv_s"]),
            jnp.stack(outs["kr_p"]), jnp.stack(outs["kr_s"]),
            jnp.stack(outs["sbk_p"]), jnp.stack(outs["sbk_s"]),
            jnp.stack(outs["sbv_p"]), jnp.stack(outs["sbv_s"]),
            jnp.stack(outs["gla_p"]), jnp.stack(outs["gla_s"]))
```

```python
import functools

import jax
import jax.numpy as jnp
from jax import lax
from jax.experimental import pallas as pl
from jax.experimental.pallas import tpu as pltpu

F32 = jnp.float32
BF16 = jnp.bfloat16

MLA_HEADS = 8
MLA_NOPE = 64
MLA_ROPE = 32
MLA_V = 64
ROPE_THETA = 10000.0
SB_HEADS = 8
SB_DIM = 64
GLA_HEADS = 4
GLA_TAU = 16.0
GLA_CHUNK = 32
N_EXPERTS = 16
N_GROUPS = 4
PER_GROUP = N_EXPERTS // N_GROUPS
NORM_EPS = 1e-6
LN_EPS = 1e-5

LANES = 128
ROPE_HALF = MLA_ROPE // 2
MLA_QPAD = 384
ROW_TILE = 512
ODD_ROW_TILE = 256
GLA_GROUP = 512
MOE_ROW_TILE = 1024
ATT_TILE = 512
SB_TILE = 256
PAGES_PER_STEP = 32
VMEM_LIMIT = 56 * 1024 * 1024


def _params(*sem):
    return pltpu.CompilerParams(dimension_semantics=sem, vmem_limit_bytes=VMEM_LIMIT)


def _dot(a, b):
    return jnp.dot(a, b, preferred_element_type=F32)


def _dot_nt(a, b):
    return lax.dot_general(a, b, (((1,), (1,)), ((), ())), preferred_element_type=F32)


def _split_dot(x, m):
    hi = x.astype(BF16)
    lo = (x - hi.astype(F32)).astype(BF16)
    return _dot(hi, m) + _dot(lo, m)


def _sigmoid(x):
    return 1.0 / (1.0 + jnp.exp(-x))


def _silu(x):
    return x * _sigmoid(x)


def _rms(x, g):
    return x * lax.rsqrt(jnp.mean(x * x, axis=-1, keepdims=True) + NORM_EPS) * g


def _layer_norm(x, g, b):
    mu = jnp.mean(x, axis=-1, keepdims=True)
    xc = x - mu
    var = jnp.mean(xc * xc, axis=-1, keepdims=True)
    return xc * lax.rsqrt(var + LN_EPS) * g + b


def _ada_kernel(c_ref, w_ref, b_ref, o_ref):
    c = c_ref[...]
    o_ref[0] = _dot(_silu(c).astype(BF16), w_ref[0].astype(BF16)) + b_ref[0]


def _ada(c_all, ada_w, ada_b):
    depth, d, n6 = ada_w.shape
    rows = c_all.shape[0]
    tn = 1024
    return pl.pallas_call(
        _ada_kernel,
        grid=(depth, n6 // tn),
        in_specs=[
            pl.BlockSpec((rows, d), lambda l, n: (0, 0)),
            pl.BlockSpec((1, d, tn), lambda l, n: (l, 0, n)),
            pl.BlockSpec((1, 1, tn), lambda l, n: (l, 0, n)),
        ],
        out_specs=pl.BlockSpec((1, rows, tn), lambda l, n: (l, 0, n)),
        out_shape=jax.ShapeDtypeStruct((depth, rows, n6), F32),
        compiler_params=_params("arbitrary", "arbitrary"),
        name="ada",
    )(c_all, ada_w, ada_b.reshape(depth, 1, n6))


def _tile_rows(tile, n, seq_len, mod):
    return min(tile, seq_len if mod.ndim == 3 else n)


def _row_spec(r, c):
    return pl.BlockSpec((r, c), lambda i: (i, 0))


def _const_spec(shape):
    nd = len(shape)
    return pl.BlockSpec(shape, lambda i: (0,) * nd)


def _mod_spec(m, r, tiles_per_seq):
    if m.ndim == 3:
        return pl.BlockSpec((None, 1, m.shape[-1]), lambda i: (i // tiles_per_seq, 0, 0))
    return _row_spec(r, m.shape[-1])


def _tab_spec(t, r, tiles_per_seq, per_token):
    if per_token:
        return _row_spec(r, t.shape[-1])
    return pl.BlockSpec((r, t.shape[-1]), lambda i: (i % tiles_per_seq, 0))


def _even_proj_kernel(x_ref, sc_ref, sh_ref, cq_ref, sq_ref, ck_ref, skn_ref, skp_ref,
                      w1_ref, qg_ref, wq_ref, kvg_ref, wcat_ref,
                      qcat_ref, kcat_ref, ckv_ref, kr_ref, sbq_ref, sbk_ref, sbv_ref,
                      sbk16_ref, sbv16_ref, *, q_lora, kv_lora, sb_w):
    h = x_ref[...] * (1.0 + sc_ref[...]) + sh_ref[...]
    y = _dot(h.astype(BF16), w1_ref[...])
    o = 0
    q_a = y[:, o:o + q_lora]; o += q_lora
    kv_a = y[:, o:o + kv_lora]; o += kv_lora
    sbq = y[:, o:o + sb_w]; o += sb_w
    sbk = y[:, o:o + sb_w]; o += sb_w
    sbv = y[:, o:o + sb_w]; o += sb_w
    kr = y[:, o:o + LANES]

    q = _dot(_rms(q_a, qg_ref[...]).astype(BF16), wq_ref[...])
    n_nope = MLA_HEADS * MLA_NOPE
    x1 = q[:, n_nope:n_nope + LANES]
    x2 = q[:, n_nope + LANES:n_nope + 2 * LANES]
    cos, sin = cq_ref[...], sq_ref[...]
    qin = jnp.concatenate([q[:, :n_nope], x1 * cos - x2 * sin, x2 * cos + x1 * sin], axis=1)
    qcat_ref[...] = _dot(qin.astype(BF16), wcat_ref[...]).astype(BF16)

    ckv = _rms(kv_a, kvg_ref[...])
    ckv_ref[...] = ckv
    kro = (kr * ck_ref[...] + pltpu.roll(kr, LANES - ROPE_HALF, 1) * skn_ref[...]
           + pltpu.roll(kr, ROPE_HALF, 1) * skp_ref[...])
    kr_ref[...] = kro[:, :MLA_ROPE]
    kcat_ref[...] = jnp.concatenate([ckv, kro], axis=1).astype(BF16)

    sbq_ref[...] = sbq.astype(BF16)
    sbk_ref[...] = sbk
    sbv_ref[...] = sbv
    sbk16_ref[...] = sbk.astype(BF16)
    sbv16_ref[...] = sbv.astype(BF16)


def _even_proj(x, sc, sh, tabs, per_token, seq_len, w1, qg, wq, kvg, wcat):
    n, d = x.shape
    r = _tile_rows(ROW_TILE, n, seq_len, sc)
    tps = max(seq_len // r, 1)
    q_lora, kv_lora = qg.shape[-1], kvg.shape[-1]
    sb_w = SB_HEADS * SB_DIM
    qw = MLA_HEADS * MLA_QPAD
    kern = functools.partial(_even_proj_kernel, q_lora=q_lora, kv_lora=kv_lora, sb_w=sb_w)
    out_shapes = [
        jax.ShapeDtypeStruct((n, qw), BF16),
        jax.ShapeDtypeStruct((n, kv_lora + LANES), BF16),
        jax.ShapeDtypeStruct((n, kv_lora), F32),
        jax.ShapeDtypeStruct((n, MLA_ROPE), F32),
        jax.ShapeDtypeStruct((n, sb_w), BF16),
        jax.ShapeDtypeStruct((n, sb_w), F32),
        jax.ShapeDtypeStruct((n, sb_w), F32),
        jax.ShapeDtypeStruct((n, sb_w), BF16),
        jax.ShapeDtypeStruct((n, sb_w), BF16),
    ]
    return pl.pallas_call(
        kern,
        grid=(n // r,),
        in_specs=[_row_spec(r, d), _mod_spec(sc, r, tps), _mod_spec(sh, r, tps)]
        + [_tab_spec(t, r, tps, per_token) for t in tabs]
        + [_const_spec(w1.shape), _const_spec(qg.shape), _const_spec(wq.shape),
           _const_spec(kvg.shape), _const_spec(wcat.shape)],
        out_specs=[_row_spec(r, s.shape[1]) for s in out_shapes],
        out_shape=out_shapes,
        compiler_params=_params("arbitrary"),
        name="even_proj",
    )(x, sc, sh, *tabs, w1, qg, wq, kvg, wcat)


def _mla_prompt_kernel(q_ref, k_ref, o_ref, m_sc, l_sc, acc_sc, *, tile, kv_lora, scale):
    qi = pl.program_id(2)
    q = q_ref[...]
    m_sc[...] = jnp.full(m_sc.shape, -jnp.inf, F32)
    l_sc[...] = jnp.zeros(l_sc.shape, F32)
    acc_sc[...] = jnp.zeros(acc_sc.shape, F32)

    def keys(kb):
        return k_ref[pl.ds(pl.multiple_of(kb * tile, tile), tile), :]

    def scores(kb):
        return _dot_nt(q, keys(kb))

    def absorb(s, kb, masked):
        s = s * scale
        if masked:
            row = lax.broadcasted_iota(jnp.int32, s.shape, 0)
            col = lax.broadcasted_iota(jnp.int32, s.shape, 1)
            s = jnp.where(col <= row, s, -jnp.inf)
        m_prev = m_sc[...]
        m_new = jnp.maximum(m_prev, jnp.max(s, axis=1, keepdims=True))
        alpha = jnp.exp(m_prev - m_new)
        p = jnp.exp(s - m_new)
        l_sc[...] = alpha * l_sc[...] + jnp.sum(p, axis=1, keepdims=True)
        acc_sc[...] = alpha * acc_sc[...] + _dot(p.astype(BF16), keys(kb)[:, :kv_lora])
        m_sc[...] = m_new

    def body(kb, s):
        s_next = scores(kb + 1)
        absorb(s, kb, False)
        return s_next

    s = lax.fori_loop(0, qi, body, scores(0))
    absorb(s, qi, True)
    o_ref[...] = (acc_sc[...] / l_sc[...]).astype(BF16)


def _mla_prompt(qcat, kcat, batch, seq):
    n = qcat.shape[0]
    kv_lora = kcat.shape[1] - LANES
    tile = min(ATT_TILE, seq)
    nq = seq // tile
    scale = (MLA_NOPE + MLA_ROPE) ** -0.5
    kern = functools.partial(_mla_prompt_kernel, tile=tile, kv_lora=kv_lora, scale=scale)
    return pl.pallas_call(
        kern,
        grid=(batch, MLA_HEADS, nq),
        in_specs=[
            pl.BlockSpec((tile, MLA_QPAD), lambda b, h, i: (b * nq + i, h)),
            pl.BlockSpec((seq, kcat.shape[1]), lambda b, h, i: (b, 0)),
        ],
        out_specs=pl.BlockSpec((tile, kv_lora), lambda b, h, i: (b * nq + i, h)),
        out_shape=jax.ShapeDtypeStruct((n, MLA_HEADS * kv_lora), BF16),
        scratch_shapes=[pltpu.VMEM((tile, 1), F32), pltpu.VMEM((tile, 1), F32),
                        pltpu.VMEM((tile, kv_lora), F32)],
        compiler_params=_params("arbitrary", "arbitrary", "arbitrary"),
        name="mla_prompt",
    )(qcat, kcat)


def _suffix_matrix(n):
    row = lax.broadcasted_iota(jnp.int32, (n, n), 0)
    col = lax.broadcasted_iota(jnp.int32, (n, n), 1)
    return jnp.where(row > col, 1.0, 0.0).astype(BF16)


def _sb_log_terms(z, valid):
    t = jnp.log(1.0 + jnp.exp(-jnp.abs(z)))
    log_beta = jnp.minimum(z, 0.0) - t
    log_stay = log_beta - z
    if valid is not None:
        log_stay = jnp.where(valid, log_stay, 0.0)
    return log_beta, log_stay


def _sb_weights(log_beta, within, carry, valid):
    w = jnp.exp(log_beta + (carry + within))
    if valid is not None:
        w = jnp.where(valid, w, 0.0)
    return w


def _sb_prompt_kernel(q_ref, k_ref, v_ref, o_ref, *, tq, tk, scale):
    qi = pl.program_id(2)
    suffix = _suffix_matrix(tk)
    q_pos = qi * tq + lax.broadcasted_iota(jnp.int32, (tq, tk), 0)
    col = lax.broadcasted_iota(jnp.int32, (tq, tk), 1)
    heads = LANES // SB_DIM
    sls = [slice(hh * SB_DIM, (hh + 1) * SB_DIM) for hh in range(heads)]
    qs = [(q_ref[:, sl].astype(F32) * scale).astype(BF16) for sl in sls]

    def rows_of(kb):
        return pl.ds(pl.multiple_of(kb * tk, tk), tk)

    def scores(kb):
        k2 = k_ref[rows_of(kb), :]
        return tuple(_dot_nt(qs[hh], k2[:, sls[hh]]) for hh in range(heads))

    def absorb(z, kb, state, masked):
        v2 = v_ref[rows_of(kb), :]
        valid = (kb * tk + col) < q_pos if masked else None
        terms = []
        for hh in range(heads):
            log_beta, log_stay = _sb_log_terms(z[hh], valid)
            terms.append((log_beta, log_stay, _split_dot(log_stay, suffix)))
        new = []
        for hh in range(heads):
            carry, acc = state[hh]
            log_beta, log_stay, within = terms[hh]
            w = _sb_weights(log_beta, within, carry, valid)
            new.append((carry + jnp.sum(log_stay, axis=1, keepdims=True),
                        acc + _dot(w.astype(BF16), v2[:, sls[hh]])))
        return tuple(new)

    top = (qi + 1) * (tq // tk) - 1
    n_masked = tq // tk
    state = tuple((jnp.zeros((tq, 1), F32), jnp.zeros((tq, SB_DIM), F32)) for _ in range(heads))
    z = scores(top)
    for i in range(n_masked):
        z_next = scores(jnp.maximum(top - 1 - i, 0))
        state = absorb(z, top - i, state, True)
        z = z_next

    def body(i, c):
        kb = top - n_masked - i
        z_next = scores(jnp.maximum(kb - 1, 0))
        return z_next, absorb(c[0], kb, c[1], False)

    z, state = lax.fori_loop(0, top + 1 - n_masked, body, (z, state))
    o_ref[...] = jnp.concatenate([acc for _, acc in state], axis=1).astype(BF16)


def _sb_prompt(sbq, sbk, sbv, batch, seq):
    n, w = sbq.shape
    tk = min(SB_TILE, seq)
    tile = min(2 * tk, seq)
    nq = seq // tile
    kern = functools.partial(_sb_prompt_kernel, tq=tile, tk=tk, scale=SB_DIM ** -0.5)
    return pl.pallas_call(
        kern,
        grid=(batch, w // LANES, nq),
        in_specs=[
            pl.BlockSpec((tile, LANES), lambda b, h, i: (b * nq + i, h)),
            pl.BlockSpec((seq, LANES), lambda b, h, i: (b, h)),
            pl.BlockSpec((seq, LANES), lambda b, h, i: (b, h)),
        ],
        out_specs=pl.BlockSpec((tile, LANES), lambda b, h, i: (b * nq + i, h)),
        out_shape=jax.ShapeDtypeStruct((n, w), BF16),
        compiler_params=_params("arbitrary", "arbitrary", "arbitrary"),
        name="sb_prompt",
    )(sbq, sbk, sbv)


def _pad_rows(x, rows):
    return jnp.concatenate([x, jnp.zeros((rows - x.shape[0], x.shape[1]), x.dtype)], axis=0)


def _sample_attention_kernel(pt_ref, q_ref, kn_ref, sq_ref, skn_ref, svn_ref, *refs,
                             n_pages, page, kv_lora, mla_scale, sb_scale, dec):
    del pt_ref
    ckv_refs = refs[:n_pages]
    krt_refs = refs[n_pages:2 * n_pages]
    sk_refs = refs[2 * n_pages:3 * n_pages]
    sv_refs = refs[3 * n_pages:4 * n_pages]
    o_ref, so_ref = refs[4 * n_pages:4 * n_pages + 2]
    (q_sc, kv_sc, krt_sc, m_sc, l_sc, acc_sc,
     sq_sc, sk_sc, sv_sc, carry_sc, sacc_sc) = refs[4 * n_pages + 2:]
    g = pl.program_id(1)
    rows = MLA_HEADS * dec
    srows = SB_HEADS * dec
    width = SB_HEADS * SB_DIM
    suffix = _suffix_matrix(page)

    def softmax_update(parts):
        m_prev = m_sc[...]
        m_new = m_prev
        for s, _ in parts:
            m_new = jnp.maximum(m_new, jnp.max(s, axis=1, keepdims=True))
        alpha = jnp.exp(m_prev - m_new)
        l_new = alpha * l_sc[...]
        acc = alpha * acc_sc[...]
        for s, values in parts:
            p = jnp.exp(s - m_new)
            l_new = l_new + jnp.sum(p, axis=1, keepdims=True)
            acc = acc + _dot(p.astype(BF16), values)
        l_sc[...] = l_new
        acc_sc[...] = acc
        m_sc[...] = m_new

    def head_mask(shape):
        r = lax.broadcasted_iota(jnp.int32, shape, 0)
        c = lax.broadcasted_iota(jnp.int32, shape, 1)
        return (r // dec) == (c // SB_DIM)

    def log_terms(z, valid, n):
        zr = jnp.concatenate([z[:, c * page:(c + 1) * page] for c in range(n)], axis=0)
        log_beta, log_stay = _sb_log_terms(zr, valid)
        return log_beta, log_stay, _split_dot(log_stay, suffix)

    def sb_weights(terms, carry, valid, n):
        log_beta, log_stay, within = terms
        total = jnp.sum(log_stay, axis=1, keepdims=True)
        carries = []
        for c in range(n):
            carries.append(carry)
            carry = carry + total[c * srows:(c + 1) * srows]
        w = _sb_weights(log_beta, within, jnp.concatenate(carries, axis=0), valid)
        w = jnp.concatenate([w[c * srows:(c + 1) * srows] for c in range(n)], axis=1)
        return w.astype(BF16), carry

    @pl.when(g == 0)
    def _():
        t = lax.broadcasted_iota(jnp.int32, (rows, page), 0) % dec
        j = lax.broadcasted_iota(jnp.int32, (rows, page), 1)
        qf = q_ref[0].astype(F32)
        q = jnp.concatenate([qf[:, h * MLA_QPAD:(h + 1) * MLA_QPAD] for h in range(MLA_HEADS)], axis=0)
        q = q.astype(BF16)
        q_sc[...] = q
        m_sc[...] = jnp.full(m_sc.shape, -jnp.inf, F32)
        l_sc[...] = jnp.zeros(l_sc.shape, F32)
        acc_sc[...] = jnp.zeros(acc_sc.shape, F32)
        kn = _pad_rows(kn_ref[0].astype(F32), page).astype(BF16)
        s = _dot_nt(q, kn) * mla_scale
        softmax_update([(jnp.where(j <= t, s, -jnp.inf), kn[:, :kv_lora])])
        sqf = sq_ref[0].astype(F32)
        sq = jnp.concatenate([sqf] * SB_HEADS, axis=0)
        sq = jnp.where(head_mask((srows, width)), sq, 0.0).astype(BF16)
        sq_sc[...] = sq
        skn = _pad_rows(skn_ref[0].astype(F32), page).astype(BF16)
        svn = _pad_rows(svn_ref[0].astype(F32), page).astype(BF16)
        valid = j < t
        w, carry = sb_weights(log_terms(_dot_nt(sq, skn) * sb_scale, valid, 1),
                              jnp.zeros((srows, 1), F32), valid, 1)
        carry_sc[...] = carry
        sacc_sc[...] = _dot(w, svn)

    q = q_sc[...]
    sq = sq_sc[...]
    half = n_pages // 2
    groups = [slice(0, half * page), slice(half * page, n_pages * page)] if half else [slice(0, n_pages * page)]
    parts, terms = [], []
    for keys in groups:
        n = (keys.stop - keys.start) // page
        for p in range(keys.start // page, keys.stop // page):
            kv_sc[p * page:(p + 1) * page, :] = ckv_refs[p][...].astype(BF16)
            krt_sc[:, p * page:(p + 1) * page] = krt_refs[p][...].astype(BF16)
        kv = kv_sc[keys, :]
        s = _dot_nt(q[:, :kv_lora], kv) + _dot(q[:, kv_lora:kv_lora + MLA_ROPE], krt_sc[:, keys])
        parts.append((s * mla_scale, kv))
        for p in range(keys.start // page, keys.stop // page):
            sk_sc[:, p * page:(p + 1) * page] = sk_refs[p][...].astype(BF16)
            sv_sc[:, p * page:(p + 1) * page] = sv_refs[p][...].astype(BF16)
        terms.append(log_terms(_dot(sq, sk_sc[:, keys]) * sb_scale, None, n))
    softmax_update(parts)
    carry = carry_sc[...]
    sacc = sacc_sc[...]
    for gi, keys in enumerate(groups):
        w, carry = sb_weights(terms[gi], carry, None, (keys.stop - keys.start) // page)
        sacc = sacc + _dot_nt(w, sv_sc[:, keys])
    carry_sc[...] = carry
    sacc_sc[...] = sacc

    @pl.when(g == pl.num_programs(1) - 1)
    def _():
        o = acc_sc[...] / l_sc[...]
        o_ref[0] = jnp.concatenate([o[h * dec:(h + 1) * dec] for h in range(MLA_HEADS)], axis=1).astype(BF16)
        sacc = jnp.where(head_mask((srows, width)), sacc_sc[...], 0.0)
        so = sacc[0:dec]
        for h in range(1, SB_HEADS):
            so = so + sacc[h * dec:(h + 1) * dec]
        so_ref[0] = so.astype(BF16)


def _page_specs(layer, n_pages, pages_total, block):
    specs = []
    for p in range(n_pages):
        def index(b, g, pt, p=p):
            return (layer, pt[b * pages_total + pages_total - 1 - (g * n_pages + p)], 0, 0)
        specs.append(pl.BlockSpec((None, None) + block, index))
    return specs


def _sample_attention(qcat, kcat_new, sbq, sbk_new, sbv_new, layer, pool_ckv, pool_krt, pool_kt, pool_vt,
                      page_table, dec):
    n, qw = qcat.shape
    width = sbq.shape[1]
    dec_batch, pages_total = page_table.shape
    page, kv_lora = pool_ckv.shape[2:]
    n_pages = min(PAGES_PER_STEP, pages_total)
    rows = MLA_HEADS * dec
    srows = SB_HEADS * dec
    kern = functools.partial(_sample_attention_kernel, n_pages=n_pages, page=page, kv_lora=kv_lora,
                             mla_scale=(MLA_NOPE + MLA_ROPE) ** -0.5, sb_scale=SB_DIM ** -0.5, dec=dec)

    def seq_spec(cols):
        return pl.BlockSpec((1, dec, cols), lambda b, g, pt: (b, 0, 0))

    grid_spec = pltpu.PrefetchScalarGridSpec(
        num_scalar_prefetch=1,
        grid=(dec_batch, pages_total // n_pages),
        in_specs=[seq_spec(qw), seq_spec(kcat_new.shape[1]), seq_spec(width), seq_spec(width), seq_spec(width)]
        + _page_specs(layer, n_pages, pages_total, (page, kv_lora))
        + _page_specs(layer, n_pages, pages_total, (MLA_ROPE, page))
        + _page_specs(layer, n_pages, pages_total, (width, page))
        + _page_specs(layer, n_pages, pages_total, (width, page)),
        out_specs=[seq_spec(MLA_HEADS * kv_lora), seq_spec(width)],
        scratch_shapes=[pltpu.VMEM((rows, MLA_QPAD), BF16),
                        pltpu.VMEM((n_pages * page, kv_lora), BF16),
                        pltpu.VMEM((MLA_ROPE, n_pages * page), BF16),
                        pltpu.VMEM((rows, 1), F32), pltpu.VMEM((rows, 1), F32),
                        pltpu.VMEM((rows, kv_lora), F32),
                        pltpu.VMEM((srows, width), BF16),
                        pltpu.VMEM((width, n_pages * page), BF16),
                        pltpu.VMEM((width, n_pages * page), BF16),
                        pltpu.VMEM((srows, 1), F32),
                        pltpu.VMEM((srows, width), F32)],
    )

    def per_seq(a):
        return a.reshape(dec_batch, dec, a.shape[1])

    olat, osb = pl.pallas_call(
        kern,
        grid_spec=grid_spec,
        out_shape=[jax.ShapeDtypeStruct((dec_batch, dec, MLA_HEADS * kv_lora), BF16),
                   jax.ShapeDtypeStruct((dec_batch, dec, width), BF16)],
        compiler_params=_params("arbitrary", "arbitrary"),
        name="sample_attention",
    )(page_table.reshape(-1), per_seq(qcat), per_seq(kcat_new), per_seq(sbq), per_seq(sbk_new),
      per_seq(sbv_new), *([pool_ckv] * n_pages), *([pool_krt] * n_pages), *([pool_kt] * n_pages),
      *([pool_vt] * n_pages))
    return olat.reshape(n, MLA_HEADS * kv_lora), osb.reshape(n, width)


def _router_gates(h2, rw_ref, rb_ref):
    logits = lax.dot_general(rw_ref[...], h2, (((1,), (1,)), ((), ())),
                             precision=lax.Precision.HIGHEST, preferred_element_type=F32)
    aff = _sigmoid(logits)
    sel = aff + rb_ref[...]
    per = PER_GROUP
    s = [sel[j * N_GROUPS:(j + 1) * N_GROUPS] for j in range(per)]
    a = [aff[j * N_GROUPS:(j + 1) * N_GROUPS] for j in range(per)]
    score = None
    for i in range(per):
        for j in range(i + 1, per):
            pair = s[i] + s[j]
            score = pair if score is None else jnp.maximum(score, pair)
    best = jnp.max(score, axis=0, keepdims=True)
    gidx = lax.broadcasted_iota(jnp.int32, score.shape, 0)
    first = jnp.min(jnp.where(score == best, gidx, N_GROUPS), axis=0, keepdims=True)
    in_group = gidx == first
    picked = []
    for j in range(per):
        rank = jnp.zeros(score.shape, F32)
        for k in range(per):
            if k == j:
                continue
            ahead = (s[k] >= s[j]) if k < j else (s[k] > s[j])
            rank = rank + jnp.where(ahead, 1.0, 0.0)
        picked.append(jnp.where(in_group, jnp.where(rank < 2.0, a[j], 0.0), 0.0))
    total = None
    for j in range(per):
        part = jnp.sum(picked[j], axis=0, keepdims=True)
        total = part if total is None else total + part
    gates_t = jnp.concatenate([p / total for p in picked], axis=0)
    gates_t = _pad_rows(gates_t, LANES)
    return gates_t.T


def _mix_tail(x, mix, g1, sc2, sh2, lng_ref, lnb_ref, rw_ref, rb_ref, alpha, x1_ref, h2_ref, gates_ref):
    x1 = _layer_norm(alpha * x + (1.0 + g1) * mix, lng_ref[...], lnb_ref[...])
    h2 = x1 * (1.0 + sc2) + sh2
    x1_ref[...] = x1
    h2_ref[...] = h2.astype(BF16)
    gates_ref[...] = _router_gates(h2, rw_ref, rb_ref)


def _even_post_kernel(olat_ref, osb_ref, x_ref, g1_ref, sc2_ref, sh2_ref, bdv_ref, wo_ref,
                      lng_ref, lnb_ref, rw_ref, rb_ref, x1_ref, h2_ref, gates_ref, *, alpha):
    o_mla = _dot(olat_ref[...], bdv_ref[...]).astype(BF16)
    nv = o_mla.shape[1]
    mix = _dot(o_mla, wo_ref[:nv, :]) + _dot(osb_ref[...], wo_ref[nv:, :])
    _mix_tail(x_ref[...], mix, g1_ref[...], sc2_ref[...], sh2_ref[...], lng_ref, lnb_ref,
              rw_ref, rb_ref, alpha, x1_ref, h2_ref, gates_ref)


def _odd_post_kernel(o_ref, r_ref, x_ref, g1_ref, sc2_ref, sh2_ref, ng_ref, wo_ref,
                     lng_ref, lnb_ref, rw_ref, rb_ref, x1_ref, h2_ref, gates_ref, *, alpha, dv):
    o = o_ref[...]
    parts = []
    for h in range(GLA_HEADS):
        sl = slice(h * dv, (h + 1) * dv)
        parts.append(_rms(o[:, sl], ng_ref[:, sl]))
    gated = jnp.concatenate(parts, axis=1) * _silu(r_ref[...])
    mix = _dot(gated.astype(BF16), wo_ref[...])
    _mix_tail(x_ref[...], mix, g1_ref[...], sc2_ref[...], sh2_ref[...], lng_ref, lnb_ref,
              rw_ref, rb_ref, alpha, x1_ref, h2_ref, gates_ref)


def _post_call(kern, name, acts, x, mods, consts, seq_len):
    n, d = x.shape
    r = _tile_rows(ROW_TILE, n, seq_len, mods[0])
    tps = max(seq_len // r, 1)
    out_shapes = [jax.ShapeDtypeStruct((n, d), F32), jax.ShapeDtypeStruct((n, d), BF16),
                  jax.ShapeDtypeStruct((n, LANES), F32)]
    return pl.pallas_call(
        kern,
        grid=(n // r,),
        in_specs=[_row_spec(r, a.shape[1]) for a in acts] + [_row_spec(r, d)]
        + [_mod_spec(m, r, tps) for m in mods] + [_const_spec(c.shape) for c in consts],
        out_specs=[_row_spec(r, s.shape[1]) for s in out_shapes],
        out_shape=out_shapes,
        compiler_params=_params("arbitrary"),
        name=name,
    )(*acts, x, *mods, *consts)


def _moe_kernel(h_ref, gates_ref, x1_ref, g2_ref, wg_ref, wu_ref, wd_ref, lng_ref, lnb_ref,
                o_ref, acc_sc, *, alpha):
    e = pl.program_id(1)

    @pl.when(e == 0)
    def _():
        acc_sc[...] = jnp.zeros(acc_sc.shape, F32)

    h = h_ref[...]
    a = _silu(_dot(h, wg_ref[0])) * _dot(h, wu_ref[0])
    y = _dot(a.astype(BF16), wd_ref[0])
    col = (e % PER_GROUP) * N_GROUPS + e // PER_GROUP
    gates = gates_ref[...]
    lane = lax.broadcasted_iota(jnp.int32, gates.shape, 1)
    gate = jnp.sum(jnp.where(lane == col, gates, 0.0), axis=1, keepdims=True)
    acc_sc[...] += gate * y

    @pl.when(e == pl.num_programs(1) - 1)
    def _():
        o_ref[...] = _layer_norm(alpha * x1_ref[...] + (1.0 + g2_ref[...]) * acc_sc[...],
                                 lng_ref[...], lnb_ref[...])


def _moe(h2, gates, x1, g2, wg, wu, wd, lng, lnb, seq_len, alpha):
    n, d = x1.shape
    r = _tile_rows(MOE_ROW_TILE, n, seq_len, g2)
    tps = max(seq_len // r, 1)
    n_exp, _, de = wg.shape
    if g2.ndim == 3:
        g2_spec = pl.BlockSpec((None, 1, d), lambda i, e: (i // tps, 0, 0))
    else:
        g2_spec = pl.BlockSpec((r, d), lambda i, e: (i, 0))
    return pl.pallas_call(
        functools.partial(_moe_kernel, alpha=alpha),
        grid=(n // r, n_exp),
        in_specs=[
            pl.BlockSpec((r, d), lambda i, e: (i, 0)),
            pl.BlockSpec((r, LANES), lambda i, e: (i, 0)),
            pl.BlockSpec((r, d), lambda i, e: (i, 0)),
            g2_spec,
            pl.BlockSpec((1, d, de), lambda i, e: (e, 0, 0)),
            pl.BlockSpec((1, d, de), lambda i, e: (e, 0, 0)),
            pl.BlockSpec((1, de, d), lambda i, e: (e, 0, 0)),
            pl.BlockSpec((1, d), lambda i, e: (0, 0)),
            pl.BlockSpec((1, d), lambda i, e: (0, 0)),
        ],
        out_specs=pl.BlockSpec((r, d), lambda i, e: (i, 0)),
        out_shape=jax.ShapeDtypeStruct((n, d), F32),
        scratch_shapes=[pltpu.VMEM((r, d), F32)],
        compiler_params=_params("arbitrary", "arbitrary"),
        name="moe",
    )(h2, gates, x1, g2, wg, wu, wd, lng, lnb)


def _odd_proj_kernel(x_ref, sc_ref, sh_ref, w_ref, wkt_ref, wgu_ref, wgut_ref, bg_ref, bgt_ref,
                     qd_ref, kdt_ref, ket_ref, dec_ref, v_ref, r_ref, *, hk, hv, chunk, qscale):
    h = (x_ref[...] * (1.0 + sc_ref[...]) + sh_ref[...]).astype(BF16)
    rows = h.shape[0]
    y = _dot(h, w_ref[...])
    q = y[:, :hk] * qscale
    v = y[:, hk:hk + hv]
    r = y[:, hk + hv:hk + 2 * hv]
    glr = y[:, hk + 2 * hv:hk + 2 * hv + LANES].astype(BF16)
    kt = _dot_nt(wkt_ref[...], h)

    def log_sigmoid(z):
        return jnp.minimum(z, 0.0) - jnp.log1p(jnp.exp(-jnp.abs(z)))

    log_a = log_sigmoid(_dot(glr, wgu_ref[...]) + bg_ref[...]) * (1.0 / GLA_TAU)
    log_at = log_sigmoid(_dot_nt(wgut_ref[...], glr) + bgt_ref[...]) * (1.0 / GLA_TAU)

    i = lax.broadcasted_iota(jnp.int32, (rows, rows), 0)
    j = lax.broadcasted_iota(jnp.int32, (rows, rows), 1)
    same = (i // chunk) == (j // chunk)
    lower = jnp.where(same & (j <= i), 1.0, 0.0).astype(BF16)
    upper = jnp.where(same & (i <= j), 1.0, 0.0).astype(BF16)
    block = jnp.where(same, 1.0, 0.0).astype(BF16)

    hi = log_a.astype(BF16)
    lo = (log_a - hi.astype(F32)).astype(BF16)
    g_cum = _dot(lower, hi) + _dot(lower, lo)
    g_cum_t = _split_dot(log_at, upper)
    g_tot_t = _split_dot(log_at, block)

    qd_ref[...] = (q * jnp.exp(g_cum)).astype(BF16)
    kdt_ref[...] = (kt * jnp.exp(-g_cum_t)).astype(BF16)
    ket_ref[...] = (kt * jnp.exp(g_tot_t - g_cum_t)).astype(BF16)
    dec_ref[...] = jnp.exp(g_tot_t)
    v_ref[...] = v.astype(BF16)
    r_ref[...] = r


def _odd_proj(x, sc, sh, seq_len, w, wkt, wgu, wgut, bg, bgt, chunk):
    n, d = x.shape
    r = _tile_rows(ODD_ROW_TILE, n, seq_len, sc)
    tps = max(seq_len // r, 1)
    hk = wkt.shape[0]
    hv = (w.shape[1] - hk - LANES) // 2
    kern = functools.partial(_odd_proj_kernel, hk=hk, hv=hv, chunk=chunk, qscale=(hk // GLA_HEADS) ** -0.5)
    col_spec = pl.BlockSpec((hk, r), lambda i: (0, i))
    out_shapes = [
        jax.ShapeDtypeStruct((n, hk), BF16),
        jax.ShapeDtypeStruct((hk, n), BF16),
        jax.ShapeDtypeStruct((hk, n), BF16),
        jax.ShapeDtypeStruct((hk, n), F32),
        jax.ShapeDtypeStruct((n, hv), BF16),
        jax.ShapeDtypeStruct((n, hv), F32),
    ]
    return pl.pallas_call(
        kern,
        grid=(n // r,),
        in_specs=[_row_spec(r, d), _mod_spec(sc, r, tps), _mod_spec(sh, r, tps)]
        + [_const_spec(c.shape) for c in (w, wkt, wgu, wgut, bg, bgt)],
        out_specs=[_row_spec(r, hk), col_spec, col_spec, col_spec, _row_spec(r, hv), _row_spec(r, hv)],
        out_shape=out_shapes,
        compiler_params=_params("arbitrary"),
        name="odd_proj",
    )(x, sc, sh, w, wkt, wgu, wgut, bg, bgt)


def _gla_chunk(qd, kdt, ket, dec, v, state, tril):
    att = jnp.where(tril, _dot(qd, kdt), 0.0)
    o = _dot(qd, state.astype(BF16)) + _dot(att.astype(BF16), v)
    return o, state * dec + _dot(ket, v)


def _gla_prompt_kernel(qd_ref, kdt_ref, ket_ref, dec_ref, v_ref, o_ref, s_ref, state_sc, *, chunk):
    g = pl.program_id(2)

    @pl.when(g == 0)
    def _():
        state_sc[...] = jnp.zeros(state_sc.shape, F32)

    i = lax.broadcasted_iota(jnp.int32, (chunk, chunk), 0)
    j = lax.broadcasted_iota(jnp.int32, (chunk, chunk), 1)
    tril = j <= i
    state = state_sc[...]
    group = qd_ref.shape[0]
    for c in range(group // chunk):
        sl = slice(c * chunk, (c + 1) * chunk)
        o, state = _gla_chunk(qd_ref[sl, :], kdt_ref[:, sl], ket_ref[:, sl],
                              dec_ref[:, c * chunk:c * chunk + 1], v_ref[sl, :], state, tril)
        o_ref[sl, :] = o
    state_sc[...] = state

    @pl.when(g == pl.num_programs(2) - 1)
    def _():
        s_ref[0, 0] = state


def _gla_prompt(qd, kdt, ket, dec, v, batch, seq, chunk):
    n, hk = qd.shape
    hv = v.shape[1]
    dk, dv = hk // GLA_HEADS, hv // GLA_HEADS
    group = min(GLA_GROUP, seq)
    ng = seq // group
    row_k = pl.BlockSpec((group, dk), lambda b, h, g: (b * ng + g, h))
    row_v = pl.BlockSpec((group, dv), lambda b, h, g: (b * ng + g, h))
    col_k = pl.BlockSpec((dk, group), lambda b, h, g: (h, b * ng + g))
    return pl.pallas_call(
        functools.partial(_gla_prompt_kernel, chunk=chunk),
        grid=(batch, GLA_HEADS, ng),
        in_specs=[row_k, col_k, col_k, col_k, row_v],
        out_specs=[row_v, pl.BlockSpec((1, 1, dk, dv), lambda b, h, g: (b, h, 0, 0))],
        out_shape=[jax.ShapeDtypeStruct((n, hv), F32),
                   jax.ShapeDtypeStruct((batch, GLA_HEADS, dk, dv), F32)],
        scratch_shapes=[pltpu.VMEM((dk, dv), F32)],
        compiler_params=_params("arbitrary", "arbitrary", "arbitrary"),
        name="gla_prompt",
    )(qd, kdt, ket, dec, v)


def _gla_sample_kernel(qd_ref, kdt_ref, ket_ref, dec_ref, v_ref, s0_ref, o_ref, s_ref, *, chunk):
    i = lax.broadcasted_iota(jnp.int32, (chunk, chunk), 0)
    j = lax.broadcasted_iota(jnp.int32, (chunk, chunk), 1)
    tril = j <= i
    group = qd_ref.shape[0]
    for c in range(group // chunk):
        sl = slice(c * chunk, (c + 1) * chunk)
        o, state = _gla_chunk(qd_ref[sl, :].astype(BF16), kdt_ref[:, sl].astype(BF16),
                              ket_ref[:, sl].astype(BF16), dec_ref[:, c * chunk:c * chunk + 1],
                              v_ref[sl, :].astype(BF16), s0_ref[c, 0], tril)
        o_ref[sl, :] = o
        s_ref[c, 0] = state


def _gla_sample(qd, kdt, ket, dec, v, s0, chunk):
    n, hk = qd.shape
    hv = v.shape[1]
    dk, dv = hk // GLA_HEADS, hv // GLA_HEADS
    group = min(LANES, n)
    seqs = group // chunk
    row_k = pl.BlockSpec((group, dk), lambda g, h: (g, h))
    row_v = pl.BlockSpec((group, dv), lambda g, h: (g, h))
    col_k = pl.BlockSpec((dk, group), lambda g, h: (h, g))
    st = pl.BlockSpec((seqs, 1, dk, dv), lambda g, h: (g, h, 0, 0))
    return pl.pallas_call(
        functools.partial(_gla_sample_kernel, chunk=chunk),
        grid=(n // group, GLA_HEADS),
        in_specs=[row_k, col_k, col_k, col_k, row_v, st],
        out_specs=[row_v, st],
        out_shape=[jax.ShapeDtypeStruct((n, hv), F32),
                   jax.ShapeDtypeStruct(s0.shape, F32)],
        compiler_params=_params("arbitrary", "arbitrary"),
        name="gla_sample",
    )(qd.astype(F32), kdt.astype(F32), ket.astype(F32), dec, v.astype(F32), s0)


def _even_weights(w_in, w_q_up, w_kv_up, q_lora, kv_lora):
    d = w_in.shape[0]
    sb_w = SB_HEADS * SB_DIM
    o = 0
    q_a = w_in[:, o:o + q_lora]; o += q_lora
    kv_a = w_in[:, o:o + kv_lora]; o += kv_lora
    k_r = w_in[:, o:o + MLA_ROPE]; o += MLA_ROPE
    sb = w_in[:, o:o + 3 * sb_w]
    w1 = jnp.concatenate([q_a, kv_a, sb, k_r, jnp.zeros((d, LANES - MLA_ROPE), w_in.dtype)], axis=1)

    qh = w_q_up.reshape(q_lora, MLA_HEADS, MLA_NOPE + MLA_ROPE)
    wq = jnp.concatenate([
        qh[:, :, :MLA_NOPE].reshape(q_lora, -1),
        qh[:, :, MLA_NOPE:MLA_NOPE + ROPE_HALF].reshape(q_lora, -1),
        qh[:, :, MLA_NOPE + ROPE_HALF:].reshape(q_lora, -1)], axis=1)

    eye_h = jnp.eye(MLA_HEADS, dtype=w_kv_up.dtype)
    eye_r = jnp.eye(ROPE_HALF, dtype=w_kv_up.dtype)
    k_up = jnp.transpose(w_kv_up[:, :, :MLA_NOPE], (1, 2, 0))
    nope_rows = (k_up[:, :, None, :] * eye_h[:, None, :, None]).reshape(MLA_HEADS * MLA_NOPE, MLA_HEADS, kv_lora)
    rope_rows = (eye_h[:, None, :, None] * eye_r[None, :, None, :]).reshape(MLA_HEADS * ROPE_HALF, MLA_HEADS, ROPE_HALF)
    pad = MLA_QPAD - kv_lora - MLA_ROPE

    def z(rows, cols):
        return jnp.zeros((rows, MLA_HEADS, cols), w_kv_up.dtype)

    n_r = MLA_HEADS * ROPE_HALF
    wcat = jnp.concatenate([
        jnp.concatenate([nope_rows, z(nope_rows.shape[0], MLA_QPAD - kv_lora)], axis=2),
        jnp.concatenate([z(n_r, kv_lora), rope_rows, z(n_r, ROPE_HALF + pad)], axis=2),
        jnp.concatenate([z(n_r, kv_lora + ROPE_HALF), rope_rows, z(n_r, pad)], axis=2)], axis=0)
    wcat = wcat.reshape(-1, MLA_HEADS * MLA_QPAD)

    v_up = jnp.transpose(w_kv_up[:, :, MLA_NOPE:], (1, 0, 2))
    bdv = (v_up[:, :, None, :] * eye_h[:, None, :, None]).reshape(MLA_HEADS * kv_lora, MLA_HEADS * MLA_V)
    return w1.astype(BF16), wq.astype(BF16), wcat.astype(BF16), bdv.astype(BF16)


def _tokens_minor(pool):
    feat = 1
    for s in pool.shape[3:]:
        feat *= s
    return jnp.moveaxis(pool, 2, -1).reshape(pool.shape[:2] + (feat, pool.shape[2]))


def _rope_tables(pos):
    inv_freq = ROPE_THETA ** (-jnp.arange(ROPE_HALF, dtype=F32) / ROPE_HALF)
    ang = pos.astype(F32)[:, None] * inv_freq[None, :]
    cos, sin = jnp.cos(ang), jnp.sin(ang)
    t = pos.shape[0]
    z16 = jnp.zeros((t, ROPE_HALF), F32)
    zpad = jnp.zeros((t, LANES - MLA_ROPE), F32)
    return (jnp.tile(cos, (1, MLA_HEADS)), jnp.tile(sin, (1, MLA_HEADS)),
            jnp.concatenate([cos, cos, zpad], axis=1),
            jnp.concatenate([-sin, z16, zpad], axis=1),
            jnp.concatenate([z16, sin, zpad], axis=1))


def _odd_weights(w_in, w_gate_up, b_gate):
    d = w_in.shape[0]
    hk = w_gate_up.shape[1]
    rank = w_gate_up.shape[0]
    hv = (w_in.shape[1] - 2 * hk - rank) // 2
    q = w_in[:, :hk]
    k = w_in[:, hk:2 * hk]
    rest = w_in[:, 2 * hk:]
    w = jnp.concatenate([q, rest, jnp.zeros((d, LANES - rank), w_in.dtype)], axis=1)
    wgu = jnp.concatenate([w_gate_up, jnp.zeros((LANES - rank, hk), w_gate_up.dtype)], axis=0)
    del hv
    return (w.astype(BF16), k.T.astype(BF16), wgu.astype(BF16), wgu.T.astype(BF16),
            b_gate.reshape(1, hk), b_gate.reshape(hk, 1))


def kernel(x_prompt, x_sample, c_prompt, c_sample, cache_mla_ckv, cache_mla_krope, cache_sb_k, cache_sb_v,
           state_gla, page_table, w_in_even, q_norm_g, w_q_up, kv_norm_g, w_kv_up, w_out_even,
           w_in_odd, w_gate_up, b_gate, gla_norm_g, w_out_odd, ada_w, ada_b, ln_g, ln_b,
           router_w, router_b, w_exp_gate, w_exp_up, w_exp_down):
    batch, seq, d = x_prompt.shape
    dec_batch, dec, _ = x_sample.shape
    depth = ada_w.shape[0]
    page = cache_mla_ckv.shape[2]
    past_len = page_table.shape[1] * page
    alpha = (2.0 * depth) ** 0.25
    n_p, n_s = batch * seq, dec_batch * dec
    q_lora, kv_lora = q_norm_g.shape[1], kv_norm_g.shape[1]
    n_exp = router_w.shape[1]
    per_group = n_exp // N_GROUPS

    c_rows = batch + dec_batch
    c_pad = -c_rows % 8
    c_all = jnp.concatenate([c_prompt, c_sample, jnp.zeros((c_pad, d), F32)], axis=0)
    mod = _ada(c_all, ada_w, ada_b)

    def mods(l):
        mp = [m.reshape(batch, 1, d) for m in jnp.split(mod[l, :batch], 6, axis=-1)]
        ms = jnp.split(jnp.repeat(mod[l, batch:c_rows], dec, axis=0), 6, axis=-1)
        return mp, ms

    perm = jnp.arange(n_exp).reshape(N_GROUPS, per_group).T.reshape(-1)
    rw = router_w.T[perm]
    rb = router_b[perm].reshape(n_exp, 1)

    pos_p = jnp.arange(seq, dtype=jnp.int32)
    pos_s = past_len + jnp.arange(dec, dtype=jnp.int32)
    tabs_p = _rope_tables(pos_p)
    tabs_s = tuple(jnp.tile(t, (dec_batch, 1)) for t in _rope_tables(pos_s))

    xp = x_prompt.reshape(n_p, d)
    xs = x_sample.reshape(n_s, d)
    outs = {k: [] for k in ("ckv_p", "ckv_s", "kr_p", "kr_s", "sbk_p", "sbk_s", "sbv_p", "sbv_s", "gla_p", "gla_s")}

    for l in range(depth):
        (sh1p, sc1p, g1p, sh2p, sc2p, g2p), (sh1s, sc1s, g1s, sh2s, sc2s, g2s) = mods(l)
        lng1, lnb1 = ln_g[l, 0].reshape(1, d), ln_b[l, 0].reshape(1, d)
        lng2, lnb2 = ln_g[l, 1].reshape(1, d), ln_b[l, 1].reshape(1, d)
        if l % 2 == 0:
            e = l // 2
            w1, wq, wcat, bdv = _even_weights(w_in_even[e], w_q_up[e], w_kv_up[e], q_lora, kv_lora)
            qg, kvg = q_norm_g[e].reshape(1, -1), kv_norm_g[e].reshape(1, -1)
            wo = w_out_even[e].astype(BF16)
            (qcat_p, kcat_p, ckv_p, kr_p, sbq_p, sbk_p, sbv_p, sbk16_p, sbv16_p) = _even_proj(
                xp, sc1p, sh1p, tabs_p, False, seq, w1, qg, wq, kvg, wcat)
            (qcat_s, kcat_s, ckv_s, kr_s, sbq_s, sbk_s, sbv_s, sbk16_s, sbv16_s) = _even_proj(
                xs, sc1s, sh1s, tabs_s, True, dec, w1, qg, wq, kvg, wcat)
            olat_p = _mla_prompt(qcat_p, kcat_p, batch, seq)
            osb_p = _sb_prompt(sbq_p, sbk16_p, sbv16_p, batch, seq)
            olat_s, osb_s = _sample_attention(
                qcat_s, kcat_s, sbq_s, sbk16_s, sbv16_s, e, cache_mla_ckv, _tokens_minor(cache_mla_krope),
                _tokens_minor(cache_sb_k), _tokens_minor(cache_sb_v), page_table, dec)
            consts = (bdv, wo, lng1, lnb1, rw, rb)
            kern = functools.partial(_even_post_kernel, alpha=alpha)
            x1p, h2p, gates_p = _post_call(kern, "even_post", (olat_p, osb_p), xp, (g1p, sc2p, sh2p), consts, seq)
            x1s, h2s, gates_s = _post_call(kern, "even_post", (olat_s, osb_s), xs, (g1s, sc2s, sh2s), consts, dec)
            outs["ckv_p"].append(ckv_p.reshape(batch, seq, kv_lora))
            outs["ckv_s"].append(ckv_s.reshape(dec_batch, dec, kv_lora))
            outs["kr_p"].append(kr_p.reshape(batch, seq, MLA_ROPE))
            outs["kr_s"].append(kr_s.reshape(dec_batch, dec, MLA_ROPE))
            outs["sbk_p"].append(sbk_p.reshape(batch, seq, SB_HEADS, SB_DIM))
            outs["sbk_s"].append(sbk_s.reshape(dec_batch, dec, SB_HEADS, SB_DIM))
            outs["sbv_p"].append(sbv_p.reshape(batch, seq, SB_HEADS, SB_DIM))
            outs["sbv_s"].append(sbv_s.reshape(dec_batch, dec, SB_HEADS, SB_DIM))
        else:
            o = l // 2
            w, wkt, wgu, wgut, bg, bgt = _odd_weights(w_in_odd[o], w_gate_up[o], b_gate[o])
            hv = gla_norm_g.shape[1] * gla_norm_g.shape[2]
            ng = gla_norm_g[o].reshape(1, hv)
            wo = w_out_odd[o].astype(BF16)
            chunk_p, chunk_s = min(GLA_CHUNK, seq), min(GLA_CHUNK, dec)
            qd, kdt, ket, dcy, v, r_p = _odd_proj(xp, sc1p, sh1p, seq, w, wkt, wgu, wgut, bg, bgt, chunk_p)
            o_p, st_p = _gla_prompt(qd, kdt, ket, dcy, v, batch, seq, chunk_p)
            qd, kdt, ket, dcy, v, r_s = _odd_proj(xs, sc1s, sh1s, dec, w, wkt, wgu, wgut, bg, bgt, chunk_s)
            o_s, st_s = _gla_sample(qd, kdt, ket, dcy, v, state_gla[o], chunk_s)
            consts = (ng, wo, lng1, lnb1, rw, rb)
            kern = functools.partial(_odd_post_kernel, alpha=alpha, dv=gla_norm_g.shape[2])
            x1p, h2p, gates_p = _post_call(kern, "odd_post", (o_p, r_p), xp, (g1p, sc2p, sh2p), consts, seq)
            x1s, h2s, gates_s = _post_call(kern, "odd_post", (o_s, r_s), xs, (g1s, sc2s, sh2s), consts, dec)
            outs["gla_p"].append(st_p)
            outs["gla_s"].append(st_s)
        wg, wu, wd = w_exp_gate[l].astype(BF16), w_exp_up[l].astype(BF16), w_exp_down[l].astype(BF16)
        xp = _moe(h2p, gates_p, x1p, g2p, wg, wu, wd, lng2, lnb2, seq, alpha)
        xs = _moe(h2s, gates_s, x1s, g2s, wg, wu, wd, lng2, lnb2, dec, alpha)

    return (xp.reshape(batch, seq, d), xs.reshape(dec_batch, dec, d),
            jnp.stack(outs["ckv_p"]), jnp.stack(outs["ckv_s"]),
            jnp.stack(outs["kr_p"]), jnp.stack(outs["kr_s"]),
            jnp.stack(outs["sbk_p"]), jnp.stack(outs["sbk_s"]),
            jnp.stack(outs["sbv_p"]), jnp.stack(outs["sbv_s"]),
            jnp.stack(outs["gla_p"]), jnp.stack(outs["gla_s"]))
```
